```python
import jax, jax.numpy as jnp
from jax import lax
import numpy as np

D_MODEL = 2048
BATCH = 4
SEQ = 2048
DEPTH = 1

CHUNK = 64
MIX_WIDTH = D_MODEL
POOL_WIDTH = MIX_WIDTH // 2
POOL_WINDOWS = (2, 4, 8, 16)
N_POOL_GROUPS = len(POOL_WINDOWS)
POOL_GROUP = POOL_WIDTH // N_POOL_GROUPS
ATTN_WIDTH = MIX_WIDTH - POOL_WIDTH
HEAD_DIM = 128
N_HEADS = ATTN_WIDTH // HEAD_DIM
Q_BLOCK = 128
D_FF = 4 * D_MODEL
PLE_DIM = 256
IN_WIDTH = POOL_WIDTH + 3 * ATTN_WIDTH + N_HEADS
EPS = 1e-6

kernel_name = "hymba_pool_fox_sandwich_block"


def rms_norm(x, g):
    xf = x.astype(jnp.float32)
    var = jnp.mean(jnp.square(xf), axis=-1, keepdims=True)
    return (xf * lax.rsqrt(var + EPS) * g.astype(jnp.float32)).astype(x.dtype)


def multiscale_pool(u, w_pool, pool_scale):
    B, S, _ = u.shape
    ug = u.reshape(B, S, N_POOL_GROUPS, POOL_GROUP)
    pos = jnp.arange(S, dtype=jnp.float32)
    outs = []
    for gi, w in enumerate(POOL_WINDOWS):
        x_g = ug[:, :, gi].astype(jnp.float32)
        cs = jnp.cumsum(x_g, axis=1)
        lag = jnp.concatenate([jnp.zeros((B, w, POOL_GROUP), jnp.float32), cs[:, :S - w]], axis=1)
        count = jnp.minimum(pos + 1.0, float(w))[None, :, None]
        outs.append((cs - lag) / count - x_g)
    y = jnp.stack(outs, axis=2).astype(u.dtype)
    y = jnp.einsum('bsgc,gcd->bsgd', y, w_pool)
    return y.reshape(B, S, POOL_WIDTH) * pool_scale


def forgetting_attention(q, k, v, log_f):
    S = q.shape[2]
    F = jnp.cumsum(log_f, axis=-1)
    scale = HEAD_DIM ** -0.5
    outs = []
    for start in range(0, S, Q_BLOCK):
        end = start + Q_BLOCK
        qb = q[:, :, start:end]
        kb = k[:, :, :end]
        vb = v[:, :, :end]
        s = jnp.einsum('bhqd,bhkd->bhqk', qb, kb).astype(jnp.float32) * scale
        s = s + F[:, :, start:end, None] - F[:, :, None, :end]
        qpos = start + jnp.arange(Q_BLOCK)
        kpos = jnp.arange(end)
        s = jnp.where(kpos[None, :] <= qpos[:, None], s, -jnp.inf)
        pr = jax.nn.softmax(s, axis=-1).astype(v.dtype)
        outs.append(jnp.einsum('bhqk,bhkd->bhqd', pr, vb))
    return jnp.concatenate(outs, axis=2)


def setup_inputs(seed: int = 0) -> dict:
    key = jax.random.key(seed)
    ks = jax.random.split(key, 24)
    f32 = jnp.float32

    def nrm(k, shape, fan_in):
        return jax.random.normal(k, shape, f32) * (fan_in ** -0.5)

    def gain(k, shape):
        return 1.0 + 0.05 * jax.random.normal(k, shape, f32)

    return {
        "x": jax.random.normal(ks[0], (BATCH, SEQ, D_MODEL), f32),
        "p": jax.random.normal(ks[1], (DEPTH, BATCH, SEQ, PLE_DIM), f32),
        "g_pre_mix": gain(ks[2], (DEPTH, D_MODEL)),
        "w_in": nrm(ks[3], (DEPTH, D_MODEL, IN_WIDTH), D_MODEL),
        "b_f": 3.0 + 0.5 * jax.random.normal(ks[4], (DEPTH, N_HEADS), f32),
        "w_pool": nrm(ks[5], (DEPTH, N_POOL_GROUPS, POOL_GROUP, POOL_GROUP), POOL_GROUP),
        "pool_scale": gain(ks[6], (DEPTH, POOL_WIDTH)),
        "g_pool_out": gain(ks[7], (DEPTH, POOL_WIDTH)),
        "g_attn_out": gain(ks[8], (DEPTH, ATTN_WIDTH)),
        "w_out": nrm(ks[9], (DEPTH, MIX_WIDTH, D_MODEL), MIX_WIDTH),
        "g_post_mix": gain(ks[10], (DEPTH, D_MODEL)),
        "g_pre_mlp": gain(ks[11], (DEPTH, D_MODEL)),
        "w_up": nrm(ks[12], (DEPTH, D_MODEL, D_FF), D_MODEL),
        "w_down": nrm(ks[13], (DEPTH, D_FF, D_MODEL), D_FF),
        "g_post_mlp": gain(ks[14], (DEPTH, D_MODEL)),
        "g_ple_gate": gain(ks[15], (DEPTH, D_MODEL)),
        "w_gate": nrm(ks[16], (DEPTH, D_MODEL, D_MODEL), D_MODEL),
        "w_ple": nrm(ks[17], (DEPTH, PLE_DIM, D_MODEL), PLE_DIM),
        "g_post_ple": gain(ks[18], (DEPTH, D_MODEL)),
    }


def reference(x, p, g_pre_mix, w_in, b_f, w_pool, pool_scale, g_pool_out, g_attn_out,
              w_out, g_post_mix, g_pre_mlp, w_up, w_down, g_post_mlp, g_ple_gate,
              w_gate, w_ple, g_post_ple):
    B, S, _ = x.shape
    h = x
    for i in range(DEPTH):
        hn = rms_norm(h, g_pre_mix[i])
        z = hn @ w_in[i]
        o = 0
        u = z[..., o:o + POOL_WIDTH]; o += POOL_WIDTH
        q = z[..., o:o + ATTN_WIDTH]; o += ATTN_WIDTH
        k = z[..., o:o + ATTN_WIDTH]; o += ATTN_WIDTH
        v = z[..., o:o + ATTN_WIDTH]; o += ATTN_WIDTH
        f_logit = z[..., o:o + N_HEADS] + b_f[i]

        y_pool = multiscale_pool(u, w_pool[i], pool_scale[i])

        to_heads = lambda t: t.reshape(B, S, N_HEADS, HEAD_DIM).transpose(0, 2, 1, 3)
        log_f = jax.nn.log_sigmoid(f_logit.astype(jnp.float32)).transpose(0, 2, 1)
        y_attn = forgetting_attention(to_heads(q), to_heads(k), to_heads(v), log_f)
        y_attn = y_attn.transpose(0, 2, 1, 3).reshape(B, S, ATTN_WIDTH)

        y = jnp.concatenate([rms_norm(y_pool, g_pool_out[i]),
                             rms_norm(y_attn, g_attn_out[i])], axis=-1)
        h = h + rms_norm(y @ w_out[i], g_post_mix[i])

        hn = rms_norm(h, g_pre_mlp[i])
        m = jnp.square(jax.nn.relu(hn @ w_up[i])) @ w_down[i]
        h = h + rms_norm(m, g_post_mlp[i])

        gate = jax.nn.sigmoid(rms_norm(h, g_ple_gate[i]) @ w_gate[i])
        e = p[i].astype(h.dtype) @ w_ple[i]
        h = h + rms_norm(gate * e, g_post_ple[i])
    return h
```

```python
import functools

import jax
import jax.numpy as jnp
from jax import lax
from jax.experimental import pallas as pl
from jax.experimental.pallas import tpu as pltpu

F32 = jnp.float32
BF16 = jnp.bfloat16

EPS = 1e-6
HEAD_DIM = 128
POOL_WINDOWS = (2, 4, 8, 16)
LANES = 128
HALO = 16

BM_PROJ = 512
BM_MIX = 512
BM_MLP = 512
BF_MLP = 1024
BM_PLE = 512
BQ = 256

VMEM_LIMIT = 56 * 1024 * 1024


def _rms(x, g):
    var = jnp.mean(x * x, axis=-1, keepdims=True)
    return x * lax.rsqrt(var + EPS) * g


def _log_sigmoid(x):
    return jnp.minimum(x, 0.0) - jnp.log1p(jnp.exp(-jnp.abs(x)))


def _resident(shape):
    return pl.BlockSpec(shape, lambda *_: (0,) * len(shape), pipeline_mode=pl.Buffered(1))


def _inproj_kernel(x_ref, g_ref, w_ref, wf_ref, bf_ref,
                   u_ref, q_ref, k_ref, v_ref, f_ref, ft_ref, carry_ref,
                   *, blocks_per_seq, pool_width, attn_width):
    i = pl.program_id(0)
    bm = x_ref.shape[0]
    hn = _rms(x_ref[...], g_ref[...]).astype(BF16)

    o = 0
    u_ref[...] = jnp.dot(hn, w_ref[:, o:o + pool_width], preferred_element_type=F32)
    o += pool_width
    for dst in (q_ref, k_ref, v_ref):
        dst[...] = jnp.dot(hn, w_ref[:, o:o + attn_width], preferred_element_type=F32).astype(BF16)
        o += attn_width

    logf = _log_sigmoid(jnp.dot(hn, wf_ref[...], preferred_element_type=F32) + bf_ref[...])
    row = lax.broadcasted_iota(jnp.int32, (bm, bm), 0)
    col = lax.broadcasted_iota(jnp.int32, (bm, bm), 1)
    tri = (col <= row).astype(BF16)
    hi = logf.astype(BF16)
    rem = logf - hi.astype(F32)
    mid = rem.astype(BF16)
    lo = (rem - mid.astype(F32)).astype(BF16)
    cs = (jnp.dot(tri, hi, preferred_element_type=F32)
          + jnp.dot(tri, mid, preferred_element_type=F32)
          + jnp.dot(tri, lo, preferred_element_type=F32))

    @pl.when(i % blocks_per_seq == 0)
    def _():
        carry_ref[...] = jnp.zeros_like(carry_ref)

    fcum = cs + carry_ref[...]
    f_ref[...] = fcum
    ft_ref[...] = fcum.T
    carry_ref[...] = fcum[bm - 1:bm, :]


def _inproj(x2, g, w_main, w_f, b_f, *, seq, pool_width, attn_width):
    rows, d = x2.shape
    bm = BM_PROJ
    blocks_per_seq = seq // bm
    n_batch = rows // seq
    row_blk = lambda w: pl.BlockSpec((bm, w), lambda i: (i, 0))
    kern = functools.partial(_inproj_kernel, blocks_per_seq=blocks_per_seq,
                             pool_width=pool_width, attn_width=attn_width)
    return pl.pallas_call(
        kern,
        grid=(rows // bm,),
        in_specs=[row_blk(d), _resident((1, d)), _resident(w_main.shape),
                  _resident(w_f.shape), _resident((1, LANES))],
        out_specs=[row_blk(pool_width), row_blk(attn_width), row_blk(attn_width), row_blk(attn_width),
                   row_blk(LANES),
                   pl.BlockSpec((None, LANES, bm), lambda i: (i // blocks_per_seq, 0, i % blocks_per_seq))],
        out_shape=[jax.ShapeDtypeStruct((rows, pool_width), F32),
                   jax.ShapeDtypeStruct((rows, attn_width), BF16),
                   jax.ShapeDtypeStruct((rows, attn_width), BF16),
                   jax.ShapeDtypeStruct((rows, attn_width), BF16),
                   jax.ShapeDtypeStruct((rows, LANES), F32),
                   jax.ShapeDtypeStruct((n_batch, LANES, seq), F32)],
        scratch_shapes=[pltpu.VMEM((1, LANES), F32)],
        compiler_params=pltpu.CompilerParams(dimension_semantics=("arbitrary",),
                                             vmem_limit_bytes=VMEM_LIMIT),
        name="inproj",
    )(x2, g, w_main, w_f, b_f)


def _attn_kernel(q_ref, k_ref, v_ref, f_ref, ft_ref, o_ref):
    h = pl.program_id(1)
    seq = q_ref.shape[0]
    scale = HEAD_DIM ** -0.5
    lane = lax.broadcasted_iota(jnp.int32, f_ref.shape, 1)
    fcol = jnp.sum(jnp.where(lane == h, f_ref[...], 0.0), axis=1, keepdims=True)
    frow = ft_ref[pl.ds(h, 1), :]
    for blk in range(seq // BQ):
        start, end = blk * BQ, (blk + 1) * BQ
        s = lax.dot_general(q_ref[start:end, :], k_ref[0:end, :], (((1,), (1,)), ((), ())),
                            preferred_element_type=F32) * scale
        s = s + fcol[start:end, :] - frow[:, 0:end]
        qpos = lax.broadcasted_iota(jnp.int32, (BQ, end), 0) + start
        kpos = lax.broadcasted_iota(jnp.int32, (BQ, end), 1)
        s = jnp.where(kpos <= qpos, s, -jnp.inf)
        m = jnp.max(s, axis=-1, keepdims=True)
        p = jnp.exp(s - m)
        denom = jnp.sum(p, axis=-1, keepdims=True)
        pr = (p / denom).astype(BF16)
        o_ref[start:end, :] = jnp.dot(pr, v_ref[0:end, :], preferred_element_type=F32)


def _attention(q3, k3, v3, f3, ft3, *, n_heads):
    n_batch, seq, width = q3.shape
    head_blk = pl.BlockSpec((None, seq, HEAD_DIM), lambda b, h: (b, 0, h))
    return pl.pallas_call(
        _attn_kernel,
        grid=(n_batch, n_heads),
        in_specs=[head_blk, head_blk, head_blk,
                  pl.BlockSpec((None, seq, LANES), lambda b, h: (b, 0, 0)),
                  pl.BlockSpec((None, 8, seq), lambda b, h: (b, 0, 0))],
        out_specs=head_blk,
        out_shape=jax.ShapeDtypeStruct((n_batch, seq, width), F32),
        compiler_params=pltpu.CompilerParams(dimension_semantics=("arbitrary", "arbitrary"),
                                             vmem_limit_bytes=VMEM_LIMIT),
        name="fox_attention",
    )(q3, k3, v3, f3, ft3)


def _mix_kernel(u_ref, halo_ref, ya_ref, x_ref, wp_ref, ps_ref, gp_ref, ga_ref, wo_ref, gm_ref,
                h_ref, ext_ref, *, blocks_per_seq, pool_group):
    i = pl.program_id(0)
    bm = u_ref.shape[0]
    pool_width = u_ref.shape[1]
    first = (i % blocks_per_seq) == 0
    ext_ref[0:HALO, :] = jnp.where(first, 0.0, halo_ref[...])
    ext_ref[HALO:HALO + bm, :] = u_ref[...]
    pos = ((i % blocks_per_seq) * bm + lax.broadcasted_iota(jnp.int32, (bm, 1), 0)).astype(F32)

    groups = []
    for gi, w in enumerate(POOL_WINDOWS):
        c0 = gi * pool_group
        cur = ext_ref[HALO:HALO + bm, c0:c0 + pool_group]
        acc = cur
        for j in range(1, w):
            acc = acc + ext_ref[HALO - j:HALO - j + bm, c0:c0 + pool_group]
        count = jnp.minimum(pos + 1.0, float(w))
        y = (acc / count - cur).astype(BF16)
        y = jnp.dot(y, wp_ref[gi], preferred_element_type=F32)
        groups.append(y * ps_ref[:, c0:c0 + pool_group])
    y_pool = jnp.concatenate(groups, axis=-1)

    n_pool = _rms(y_pool, gp_ref[...]).astype(BF16)
    n_attn = _rms(ya_ref[...], ga_ref[...]).astype(BF16)
    mixed = (jnp.dot(n_pool, wo_ref[0:pool_width, :], preferred_element_type=F32)
             + jnp.dot(n_attn, wo_ref[pool_width:, :], preferred_element_type=F32))
    h_ref[...] = x_ref[...] + _rms(mixed, gm_ref[...])


def _mix(u, y_attn, x2, w_pool, pool_scale, g_pool, g_attn, w_out, g_post, *, seq):
    rows, d = x2.shape
    pool_width = u.shape[1]
    attn_width = y_attn.shape[1]
    bm = BM_MIX
    blocks_per_seq = seq // bm
    row_blk = lambda w: pl.BlockSpec((bm, w), lambda i: (i, 0))
    halo_blk = pl.BlockSpec((HALO, pool_width), lambda i: (jnp.maximum(i * (bm // HALO) - 1, 0), 0))
    kern = functools.partial(_mix_kernel, blocks_per_seq=blocks_per_seq, pool_group=w_pool.shape[-1])
    return pl.pallas_call(
        kern,
        grid=(rows // bm,),
        in_specs=[row_blk(pool_width), halo_blk, row_blk(attn_width), row_blk(d),
                  _resident(w_pool.shape), _resident((1, pool_width)), _resident((1, pool_width)),
                  _resident((1, attn_width)), _resident(w_out.shape), _resident((1, d))],
        out_specs=row_blk(d),
        out_shape=jax.ShapeDtypeStruct((rows, d), F32),
        scratch_shapes=[pltpu.VMEM((HALO + bm, pool_width), F32)],
        compiler_params=pltpu.CompilerParams(dimension_semantics=("arbitrary",),
                                             vmem_limit_bytes=VMEM_LIMIT),
        name="mix_out",
    )(u, u, y_attn, x2, w_pool, pool_scale, g_pool, g_attn, w_out, g_post)


def _mlp_kernel(h_ref, gpre_ref, wu_ref, wd_ref, gpost_ref, o_ref, hn_ref):
    j = pl.program_id(1)

    @pl.when(j == 0)
    def _():
        hn_ref[...] = _rms(h_ref[...], gpre_ref[...]).astype(BF16)

    a = jnp.maximum(jnp.dot(hn_ref[...], wu_ref[...], preferred_element_type=F32), 0.0)
    part = jnp.dot((a * a).astype(BF16), wd_ref[...], preferred_element_type=F32)

    @pl.when(j == 0)
    def _():
        o_ref[...] = part

    @pl.when(j > 0)
    def _():
        o_ref[...] += part

    @pl.when(j == pl.num_programs(1) - 1)
    def _():
        o_ref[...] = h_ref[...] + _rms(o_ref[...], gpost_ref[...])


def _mlp(h1, g_pre, w_up, w_down, g_post):
    rows, d = h1.shape
    d_ff = w_up.shape[1]
    bm, bf = BM_MLP, BF_MLP
    return pl.pallas_call(
        _mlp_kernel,
        grid=(rows // bm, d_ff // bf),
        in_specs=[pl.BlockSpec((bm, d), lambda i, j: (i, 0)),
                  _resident((1, d)),
                  pl.BlockSpec((d, bf), lambda i, j: (0, j)),
                  pl.BlockSpec((bf, d), lambda i, j: (j, 0)),
                  _resident((1, d))],
        out_specs=pl.BlockSpec((bm, d), lambda i, j: (i, 0)),
        out_shape=jax.ShapeDtypeStruct((rows, d), F32),
        scratch_shapes=[pltpu.VMEM((bm, d), BF16)],
        compiler_params=pltpu.CompilerParams(dimension_semantics=("arbitrary", "arbitrary"),
                                             vmem_limit_bytes=VMEM_LIMIT),
        name="sqrelu_mlp",
    )(h1, g_pre, w_up, w_down, g_post)


def _ple_kernel(h_ref, p_ref, gg_ref, wg_ref, wp_ref, gpost_ref, o_ref):
    h = h_ref[...]
    hn = _rms(h, gg_ref[...]).astype(BF16)
    gate = jax.nn.sigmoid(jnp.dot(hn, wg_ref[...], preferred_element_type=F32))
    e = jnp.dot(p_ref[...].astype(BF16), wp_ref[...], preferred_element_type=F32)
    o_ref[...] = h + _rms(gate * e, gpost_ref[...])


def _ple(h2, p2, g_gate, w_gate, w_ple, g_post):
    rows, d = h2.shape
    bm = BM_PLE
    return pl.pallas_call(
        _ple_kernel,
        grid=(rows // bm,),
        in_specs=[pl.BlockSpec((bm, d), lambda i: (i, 0)),
                  pl.BlockSpec((bm, p2.shape[1]), lambda i: (i, 0)),
                  _resident((1, d)), _resident(w_gate.shape), _resident(w_ple.shape), _resident((1, d))],
        out_specs=pl.BlockSpec((bm, d), lambda i: (i, 0)),
        out_shape=jax.ShapeDtypeStruct((rows, d), F32),
        compiler_params=pltpu.CompilerParams(dimension_semantics=("arbitrary",),
                                             vmem_limit_bytes=VMEM_LIMIT),
        name="ple_gate",
    )(h2, p2, g_gate, w_gate, w_ple, g_post)


def kernel(x, p, g_pre_mix, w_in, b_f, w_pool, pool_scale, g_pool_out, g_attn_out, w_out, g_post_mix,
           g_pre_mlp, w_up, w_down, g_post_mlp, g_ple_gate, w_gate, w_ple, g_post_ple):
    n_batch, seq, d = x.shape
    depth = w_in.shape[0]
    pool_width = pool_scale.shape[-1]
    attn_width = g_attn_out.shape[-1]
    n_heads = b_f.shape[-1]
    main_cols = pool_width + 3 * attn_width
    rows = n_batch * seq
    vec = lambda a: a.reshape(1, -1)

    h = x.reshape(rows, d)
    for i in range(depth):
        w_main = w_in[i, :, :main_cols].astype(BF16)
        w_f = jnp.pad(w_in[i, :, main_cols:], ((0, 0), (0, LANES - n_heads))).astype(BF16)
        bias_f = jnp.pad(b_f[i], (0, LANES - n_heads)).reshape(1, LANES)

        u, q, k, v, fcum, fcum_t = _inproj(h, vec(g_pre_mix[i]), w_main, w_f, bias_f,
                                           seq=seq, pool_width=pool_width, attn_width=attn_width)
        to3 = lambda a: a.reshape(n_batch, seq, a.shape[-1])
        y_attn = _attention(to3(q), to3(k), to3(v), to3(fcum), fcum_t, n_heads=n_heads)
        h = _mix(u, y_attn.reshape(rows, attn_width), h, w_pool[i].astype(BF16), vec(pool_scale[i]),
                 vec(g_pool_out[i]), vec(g_attn_out[i]), w_out[i].astype(BF16), vec(g_post_mix[i]), seq=seq)
        h = _mlp(h, vec(g_pre_mlp[i]), w_up[i].astype(BF16), w_down[i].astype(BF16), vec(g_post_mlp[i]))
        h = _ple(h, p[i].reshape(rows, -1), vec(g_ple_gate[i]), w_gate[i].astype(BF16),
                 w_ple[i].astype(BF16), vec(g_post_ple[i]))
    return h.reshape(n_batch, seq, d)
```

```python
import functools

import jax
import jax.numpy as jnp
from jax import lax
from jax.experimental import pallas as pl
from jax.experimental.pallas import tpu as pltpu

F32 = jnp.float32
BF16 = jnp.bfloat16

EPS = 1e-6
HEAD_DIM = 128
POOL_WINDOWS = (2, 4, 8, 16)
LANES = 128
HALO = 16

BM_PROJ = 1024
BN_PROJ = 512
CUMSUM_BLOCK = 256
BM_MIX = 512
BM_MLP = 1024
BF_MLP = 512
BM_PLE = 512
BQ = 256

VMEM_LIMIT = 56 * 1024 * 1024


def _rms(x, g):
    var = jnp.mean(x * x, axis=-1, keepdims=True)
    return x * lax.rsqrt(var + EPS) * g


def _log_sigmoid(x):
    return jnp.minimum(x, 0.0) - jnp.log1p(jnp.exp(-jnp.abs(x)))


def _resident(shape):
    return pl.BlockSpec(shape, lambda *_: (0,) * len(shape), pipeline_mode=pl.Buffered(1))


def _inproj_kernel(x_ref, g_ref, w_ref, wf_ref, bf_ref, wo_ref, wg_ref,
                   u_ref, qkv_ref, f_ref, ft_ref, wo_bf_ref, wg_bf_ref,
                   hn_ref, carry_ref, *, blocks_per_seq, u_tiles):
    i = pl.program_id(0)
    j = pl.program_id(1)
    bm = x_ref.shape[0]

    wo_bf_ref[...] = wo_ref[...].astype(BF16)
    wg_bf_ref[...] = wg_ref[...].astype(BF16)

    @pl.when(j == 0)
    def _():
        hn = _rms(x_ref[...], g_ref[...]).astype(BF16)
        hn_ref[...] = hn
        logf = _log_sigmoid(jnp.dot(hn, wf_ref[...].astype(BF16), preferred_element_type=F32) + bf_ref[...])
        n = CUMSUM_BLOCK
        row = lax.broadcasted_iota(jnp.int32, (n, n), 0)
        col = lax.broadcasted_iota(jnp.int32, (n, n), 1)
        tri = (col <= row).astype(BF16)
        carry = jnp.where(i % blocks_per_seq == 0, 0.0, carry_ref[...])
        for sb in range(bm // n):
            piece = logf[sb * n:(sb + 1) * n, :]
            hi = piece.astype(BF16)
            rem = piece - hi.astype(F32)
            mid = rem.astype(BF16)
            lo = (rem - mid.astype(F32)).astype(BF16)
            cs = (jnp.dot(tri, hi, preferred_element_type=F32)
                  + jnp.dot(tri, mid, preferred_element_type=F32)
                  + jnp.dot(tri, lo, preferred_element_type=F32)) + carry
            f_ref[sb * n:(sb + 1) * n, :] = cs
            ft_ref[:, sb * n:(sb + 1) * n] = cs.T
            carry = cs[n - 1:n, :]
        carry_ref[...] = carry

    @pl.when(j < u_tiles)
    def _():
        u_ref[...] = jnp.dot(hn_ref[...], w_ref[...].astype(BF16), preferred_element_type=F32)

    @pl.when(j >= u_tiles)
    def _():
        qkv_ref[...] = jnp.dot(hn_ref[...], w_ref[...].astype(BF16), preferred_element_type=F32).astype(BF16)


def _inproj(x2, g, w_in, w_f, b_f, w_out, w_gate, *, seq, pool_width, attn_width):
    rows, d = x2.shape
    bm, bn = BM_PROJ, BN_PROJ
    blocks_per_seq = seq // bm
    n_batch = rows // seq
    u_tiles = pool_width // bn
    n_tiles = (pool_width + 3 * attn_width) // bn
    n_steps = (rows // bm) * n_tiles
    cast_rows = w_out.shape[0] // n_steps
    cast_blk = pl.BlockSpec((cast_rows, w_out.shape[1]), lambda i, j: (i * n_tiles + j, 0))
    kern = functools.partial(_inproj_kernel, blocks_per_seq=blocks_per_seq, u_tiles=u_tiles)
    return pl.pallas_call(
        kern,
        grid=(rows // bm, n_tiles),
        in_specs=[pl.BlockSpec((bm, d), lambda i, j: (i, 0)),
                  _resident((1, d)),
                  pl.BlockSpec((d, bn), lambda i, j: (0, j)),
                  _resident(w_f.shape), _resident((1, LANES)),
                  cast_blk, cast_blk],
        out_specs=[pl.BlockSpec((bm, bn), lambda i, j: (i, jnp.minimum(j, u_tiles - 1))),
                   pl.BlockSpec((bm, bn), lambda i, j: (i, jnp.maximum(j - u_tiles, 0))),
                   pl.BlockSpec((bm, LANES), lambda i, j: (i, 0)),
                   pl.BlockSpec((None, LANES, bm), lambda i, j: (i // blocks_per_seq, 0, i % blocks_per_seq)),
                   cast_blk, cast_blk],
        out_shape=[jax.ShapeDtypeStruct((rows, pool_width), F32),
                   jax.ShapeDtypeStruct((rows, 3 * attn_width), BF16),
                   jax.ShapeDtypeStruct((rows, LANES), F32),
                   jax.ShapeDtypeStruct((n_batch, LANES, seq), F32),
                   jax.ShapeDtypeStruct(w_out.shape, BF16),
                   jax.ShapeDtypeStruct(w_gate.shape, BF16)],
        scratch_shapes=[pltpu.VMEM((bm, d), BF16), pltpu.VMEM((1, LANES), F32)],
        compiler_params=pltpu.CompilerParams(dimension_semantics=("arbitrary", "arbitrary"),
                                             vmem_limit_bytes=VMEM_LIMIT),
        name="inproj",
    )(x2, g, w_in, w_f, b_f, w_out, w_gate)


def _attn_kernel(q_ref, k_ref, v_ref, f_ref, ft_ref, o_ref):
    h = pl.program_id(1)
    seq = q_ref.shape[0]
    scale = HEAD_DIM ** -0.5
    lane = lax.broadcasted_iota(jnp.int32, f_ref.shape, 1)
    fcol = jnp.sum(jnp.where(lane == h, f_ref[...], 0.0), axis=1, keepdims=True)
    frow = ft_ref[pl.ds(h, 1), :]
    for blk in range(seq // BQ):
        start, end = blk * BQ, (blk + 1) * BQ
        s = lax.dot_general(q_ref[start:end, :], k_ref[0:end, :], (((1,), (1,)), ((), ())),
                            preferred_element_type=F32) * scale
        s = s + fcol[start:end, :] - frow[:, 0:end]
        qpos = lax.broadcasted_iota(jnp.int32, (BQ, end), 0) + start
        kpos = lax.broadcasted_iota(jnp.int32, (BQ, end), 1)
        s = jnp.where(kpos <= qpos, s, -jnp.inf)
        m = jnp.max(s, axis=-1, keepdims=True)
        p = jnp.exp(s - m)
        denom = jnp.sum(p, axis=-1, keepdims=True)
        pr = (p / denom).astype(BF16)
        o_ref[start:end, :] = jnp.dot(pr, v_ref[0:end, :], preferred_element_type=F32)


def _attention(qkv3, f3, ft3, *, n_heads):
    n_batch, seq, _ = qkv3.shape
    head_blk = lambda part: pl.BlockSpec((None, seq, HEAD_DIM), lambda b, h: (b, 0, part * n_heads + h))
    return pl.pallas_call(
        _attn_kernel,
        grid=(n_batch, n_heads),
        in_specs=[head_blk(0), head_blk(1), head_blk(2),
                  pl.BlockSpec((None, seq, LANES), lambda b, h: (b, 0, 0)),
                  pl.BlockSpec((None, 8, seq), lambda b, h: (b, 0, 0))],
        out_specs=head_blk(0),
        out_shape=jax.ShapeDtypeStruct((n_batch, seq, n_heads * HEAD_DIM), F32),
        compiler_params=pltpu.CompilerParams(dimension_semantics=("arbitrary", "arbitrary"),
                                             vmem_limit_bytes=VMEM_LIMIT),
        name="fox_attention",
    )(qkv3, qkv3, qkv3, f3, ft3)


def _mix_kernel(u_ref, halo_ref, ya_ref, x_ref, wp_ref, ps_ref, gp_ref, ga_ref, wo_ref, gm_ref, gn_ref,
                h_ref, hn_ref, ext_ref, *, blocks_per_seq, pool_group):
    i = pl.program_id(0)
    bm = u_ref.shape[0]
    pool_width = u_ref.shape[1]
    first = (i % blocks_per_seq) == 0
    ext_ref[0:HALO, :] = jnp.where(first, 0.0, halo_ref[...])
    ext_ref[HALO:HALO + bm, :] = u_ref[...]
    pos = ((i % blocks_per_seq) * bm + lax.broadcasted_iota(jnp.int32, (bm, 1), 0)).astype(F32)

    groups = []
    for gi, w in enumerate(POOL_WINDOWS):
        c0 = gi * pool_group
        cur = ext_ref[HALO:HALO + bm, c0:c0 + pool_group]
        acc = cur
        for j in range(1, w):
            acc = acc + ext_ref[HALO - j:HALO - j + bm, c0:c0 + pool_group]
        count = jnp.minimum(pos + 1.0, float(w))
        y = (acc / count - cur).astype(BF16)
        y = jnp.dot(y, wp_ref[gi].astype(BF16), preferred_element_type=F32)
        groups.append(y * ps_ref[:, c0:c0 + pool_group])
    y_pool = jnp.concatenate(groups, axis=-1)

    n_pool = _rms(y_pool, gp_ref[...]).astype(BF16)
    n_attn = _rms(ya_ref[...], ga_ref[...]).astype(BF16)
    mixed = (jnp.dot(n_pool, wo_ref[0:pool_width, :], preferred_element_type=F32)
             + jnp.dot(n_attn, wo_ref[pool_width:, :], preferred_element_type=F32))
    h1 = x_ref[...] + _rms(mixed, gm_ref[...])
    h_ref[...] = h1
    hn_ref[...] = _rms(h1, gn_ref[...]).astype(BF16)


def _mix(u, y_attn, x2, w_pool, pool_scale, g_pool, g_attn, w_out_bf, g_post, g_next, *, seq):
    rows, d = x2.shape
    pool_width = u.shape[1]
    attn_width = y_attn.shape[1]
    bm = BM_MIX
    blocks_per_seq = seq // bm
    row_blk = lambda w: pl.BlockSpec((bm, w), lambda i: (i, 0))
    halo_blk = pl.BlockSpec((HALO, pool_width), lambda i: (jnp.maximum(i * (bm // HALO) - 1, 0), 0))
    kern = functools.partial(_mix_kernel, blocks_per_seq=blocks_per_seq, pool_group=w_pool.shape[-1])
    return pl.pallas_call(
        kern,
        grid=(rows // bm,),
        in_specs=[row_blk(pool_width), halo_blk, row_blk(attn_width), row_blk(d),
                  _resident(w_pool.shape), _resident((1, pool_width)), _resident((1, pool_width)),
                  _resident((1, attn_width)), _resident(w_out_bf.shape), _resident((1, d)), _resident((1, d))],
        out_specs=[row_blk(d), row_blk(d)],
        out_shape=[jax.ShapeDtypeStruct((rows, d), F32), jax.ShapeDtypeStruct((rows, d), BF16)],
        scratch_shapes=[pltpu.VMEM((HALO + bm, pool_width), F32)],
        compiler_params=pltpu.CompilerParams(dimension_semantics=("arbitrary",),
                                             vmem_limit_bytes=VMEM_LIMIT),
        name="mix_out",
    )(u, u, y_attn, x2, w_pool, pool_scale, g_pool, g_attn, w_out_bf, g_post, g_next)


def _mlp_kernel(hn_ref, wu_ref, wd_ref, o_ref):
    @pl.when(pl.program_id(1) == 0)
    def _():
        o_ref[...] = jnp.zeros_like(o_ref)

    a = jnp.maximum(jnp.dot(hn_ref[...], wu_ref[...].astype(BF16), preferred_element_type=F32), 0.0)
    o_ref[...] += jnp.dot((a * a).astype(BF16), wd_ref[...].astype(BF16), preferred_element_type=F32)


def _mlp(hn, w_up, w_down):
    rows, d = hn.shape
    d_ff = w_up.shape[1]
    bm, bf = BM_MLP, BF_MLP
    return pl.pallas_call(
        _mlp_kernel,
        grid=(rows // bm, d_ff // bf),
        in_specs=[pl.BlockSpec((bm, d), lambda i, j: (i, 0)),
                  pl.BlockSpec((d, bf), lambda i, j: (0, j)),
                  pl.BlockSpec((bf, d), lambda i, j: (j, 0))],
        out_specs=pl.BlockSpec((bm, d), lambda i, j: (i, 0)),
        out_shape=jax.ShapeDtypeStruct((rows, d), F32),
        compiler_params=pltpu.CompilerParams(dimension_semantics=("arbitrary", "arbitrary"),
                                             vmem_limit_bytes=VMEM_LIMIT),
        name="sqrelu_mlp",
    )(hn, w_up, w_down)


def _ple_kernel(h_ref, m_ref, p_ref, gm_ref, gg_ref, wg_ref, wp_ref, gpost_ref, o_ref):
    h = h_ref[...] + _rms(m_ref[...], gm_ref[...])
    hn = _rms(h, gg_ref[...]).astype(BF16)
    gate = jax.nn.sigmoid(jnp.dot(hn, wg_ref[...], preferred_element_type=F32))
    e = jnp.dot(p_ref[...].astype(BF16), wp_ref[...].astype(BF16), preferred_element_type=F32)
    o_ref[...] = h + _rms(gate * e, gpost_ref[...])


def _ple(h1, m, p2, g_post_mlp, g_gate, w_gate_bf, w_ple, g_post):
    rows, d = h1.shape
    bm = BM_PLE
    row_blk = lambda w: pl.BlockSpec((bm, w), lambda i: (i, 0))
    return pl.pallas_call(
        _ple_kernel,
        grid=(rows // bm,),
        in_specs=[row_blk(d), row_blk(d), row_blk(p2.shape[1]),
                  _resident((1, d)), _resident((1, d)), _resident(w_gate_bf.shape), _resident(w_ple.shape),
                  _resident((1, d))],
        out_specs=row_blk(d),
        out_shape=jax.ShapeDtypeStruct((rows, d), F32),
        compiler_params=pltpu.CompilerParams(dimension_semantics=("arbitrary",),
                                             vmem_limit_bytes=VMEM_LIMIT),
        name="ple_gate",
    )(h1, m, p2, g_post_mlp, g_gate, w_gate_bf, w_ple, g_post)


def kernel(x, p, g_pre_mix, w_in, b_f, w_pool, pool_scale, g_pool_out, g_attn_out, w_out, g_post_mix,
           g_pre_mlp, w_up, w_down, g_post_mlp, g_ple_gate, w_gate, w_ple, g_post_ple):
    n_batch, seq, d = x.shape
    depth = w_in.shape[0]
    pool_width = pool_scale.shape[-1]
    attn_width = g_attn_out.shape[-1]
    n_heads = b_f.shape[-1]
    main_cols = pool_width + 3 * attn_width
    rows = n_batch * seq
    vec = lambda a: a.reshape(1, -1)

    h = x.reshape(rows, d)
    for i in range(depth):
        w_f = jnp.pad(w_in[i, :, main_cols:], ((0, 0), (0, LANES - n_heads)))
        bias_f = jnp.pad(b_f[i], (0, LANES - n_heads)).reshape(1, LANES)

        u, qkv, fcum, fcum_t, w_out_bf, w_gate_bf = _inproj(
            h, vec(g_pre_mix[i]), w_in[i], w_f, bias_f, w_out[i], w_gate[i],
            seq=seq, pool_width=pool_width, attn_width=attn_width)
        y_attn = _attention(qkv.reshape(n_batch, seq, -1), fcum.reshape(n_batch, seq, LANES), fcum_t,
                            n_heads=n_heads)
        h1, hn = _mix(u, y_attn.reshape(rows, attn_width), h, w_pool[i], vec(pool_scale[i]),
                      vec(g_pool_out[i]), vec(g_attn_out[i]), w_out_bf, vec(g_post_mix[i]),
                      vec(g_pre_mlp[i]), seq=seq)
        m = _mlp(hn, w_up[i], w_down[i])
        h = _ple(h1, m, p[i].reshape(rows, -1), vec(g_post_mlp[i]), vec(g_ple_gate[i]), w_gate_bf,
                 w_ple[i], vec(g_post_ple[i]))
    return h.reshape(n_batch, seq, d)
```

```python
import functools

import jax
import jax.numpy as jnp
from jax import lax
from jax.experimental import pallas as pl
from jax.experimental.pallas import tpu as pltpu

F32 = jnp.float32
BF16 = jnp.bfloat16

EPS = 1e-6
HEAD_DIM = 128
POOL_WINDOWS = (2, 4, 8, 16)
LANES = 128
HALO = 16
LOG2E = 1.4426950408889634
QUERY_SCALE = HEAD_DIM ** -0.5 * LOG2E
FGATE_PIECES = 3

BM_PROJ = 1024
BN_PROJ = 512
CUMSUM_BLOCK = 256
BM_MIX = 512
BM_MLP = 1024
BF_MLP = 512
BM_PLE = 512
BQ = 512

VMEM_LIMIT = 56 * 1024 * 1024


def _rms(x, g):
    var = jnp.mean(x * x, axis=-1, keepdims=True)
    return x * lax.rsqrt(var + EPS) * g


def _log_sigmoid(x):
    return jnp.minimum(x, 0.0) - jnp.log1p(jnp.exp(-jnp.abs(x)))


def _resident(shape):
    return pl.BlockSpec(shape, lambda *_: (0,) * len(shape), pipeline_mode=pl.Buffered(1))


def _dot_t(a, b_t):
    return lax.dot_general(a, b_t, (((1,), (1,)), ((), ())), preferred_element_type=F32)


def _inproj_kernel(x_ref, g_ref, w_ref, wf_ref, bf_ref, wo_ref, wg_ref,
                   u_ref, qkv_ref, fa_ref, wo_bf_ref, wg_bf_ref,
                   hn_ref, carry_ref, *, blocks_per_seq, u_tiles, q_tiles, n_heads):
    i = pl.program_id(0)
    j = pl.program_id(1)
    bm, d = x_ref.shape

    wo_bf_ref[...] = wo_ref[...].astype(BF16)
    wg_bf_ref[...] = wg_ref[...].astype(BF16)

    @pl.when(j == 0)
    def _():
        hn = _rms(x_ref[...], g_ref[...]).astype(BF16)
        hn_ref[...] = hn
        wf = wf_ref[...]
        wf_rep = jnp.concatenate([wf] * FGATE_PIECES + [jnp.zeros((LANES - FGATE_PIECES * n_heads, d), F32)],
                                 axis=0).astype(BF16)
        logf = _log_sigmoid(_dot_t(hn, wf_rep) + bf_ref[...])
        n = CUMSUM_BLOCK
        row = lax.broadcasted_iota(jnp.int32, (n, n), 0)
        col = lax.broadcasted_iota(jnp.int32, (n, n), 1)
        tri = (col <= row).astype(BF16)
        lane = lax.broadcasted_iota(jnp.int32, (n, LANES), 1)
        carry = jnp.where(i % blocks_per_seq == 0, 0.0, carry_ref[...])
        for sb in range(bm // n):
            hi, mid, lo = _split3(logf[sb * n:(sb + 1) * n, :])
            cs = (jnp.dot(tri, hi, preferred_element_type=F32)
                  + jnp.dot(tri, mid, preferred_element_type=F32)
                  + jnp.dot(tri, lo, preferred_element_type=F32)) + carry
            carry = cs[n - 1:n, :]
            hi, mid, lo = _split3(cs * (-LOG2E))
            zero = jnp.zeros_like(hi)
            fa_ref[sb * n:(sb + 1) * n, :] = jnp.where(
                lane < n_heads, hi, jnp.where(lane < 2 * n_heads, mid, jnp.where(lane < 3 * n_heads, lo, zero)))
        carry_ref[...] = carry

    @pl.when(j < u_tiles)
    def _():
        u_ref[...] = _dot_t(hn_ref[...], w_ref[...].astype(BF16))

    @pl.when(j >= u_tiles)
    def _():
        c = jnp.where(j < u_tiles + q_tiles, QUERY_SCALE, 1.0)
        qkv_ref[...] = (_dot_t(hn_ref[...], w_ref[...].astype(BF16)) * c).astype(BF16)


def _split3(v):
    hi = v.astype(BF16)
    rem = v - hi.astype(F32)
    mid = rem.astype(BF16)
    lo = (rem - mid.astype(F32)).astype(BF16)
    return hi, mid, lo


def _inproj(x2, g, w_in_t, b_f, w_out, w_gate, *, seq, pool_width, attn_width, n_heads):
    rows, d = x2.shape
    bm, bn = BM_PROJ, BN_PROJ
    assert n_heads == 8 and FGATE_PIECES * n_heads <= LANES
    blocks_per_seq = seq // bm
    main_cols = pool_width + 3 * attn_width
    u_tiles = pool_width // bn
    q_tiles = attn_width // bn
    n_tiles = main_cols // bn
    n_steps = (rows // bm) * n_tiles
    cast_rows = w_out.shape[0] // n_steps
    cast_blk = pl.BlockSpec((cast_rows, w_out.shape[1]), lambda i, j: (i * n_tiles + j, 0))
    kern = functools.partial(_inproj_kernel, blocks_per_seq=blocks_per_seq, u_tiles=u_tiles, q_tiles=q_tiles,
                             n_heads=n_heads)
    return pl.pallas_call(
        kern,
        grid=(rows // bm, n_tiles),
        in_specs=[pl.BlockSpec((bm, d), lambda i, j: (i, 0)),
                  _resident((1, d)),
                  pl.BlockSpec((bn, d), lambda i, j: (j, 0)),
                  pl.BlockSpec((n_heads, d), lambda i, j: (main_cols // n_heads, 0)),
                  _resident((1, LANES)),
                  cast_blk, cast_blk],
        out_specs=[pl.BlockSpec((bm, bn), lambda i, j: (i, jnp.minimum(j, u_tiles - 1))),
                   pl.BlockSpec((bm, bn), lambda i, j: (i, jnp.maximum(j - u_tiles, 0))),
                   pl.BlockSpec((bm, LANES), lambda i, j: (i, 0)),
                   cast_blk, cast_blk],
        out_shape=[jax.ShapeDtypeStruct((rows, pool_width), F32),
                   jax.ShapeDtypeStruct((rows, 3 * attn_width), BF16),
                   jax.ShapeDtypeStruct((rows, LANES), BF16),
                   jax.ShapeDtypeStruct(w_out.shape, BF16),
                   jax.ShapeDtypeStruct(w_gate.shape, BF16)],
        scratch_shapes=[pltpu.VMEM((bm, d), BF16), pltpu.VMEM((1, LANES), F32)],
        compiler_params=pltpu.CompilerParams(dimension_semantics=("arbitrary", "arbitrary"),
                                             vmem_limit_bytes=VMEM_LIMIT),
        name="inproj",
    )(x2, g, w_in_t, w_in_t, b_f, w_out, w_gate)


def _attn_kernel(q_ref, k_ref, v_ref, fa_ref, o_ref, kaug_ref, vt_ref, *, n_heads):
    h = pl.program_id(1)
    seq = q_ref.shape[0]
    kaug_ref[:, 0:HEAD_DIM] = k_ref[...]
    kaug_ref[:, HEAD_DIM:] = fa_ref[...]
    vt_ref[...] = v_ref[...].T
    lane = lax.broadcasted_iota(jnp.int32, (BQ, LANES), 1)
    pick = (lane == h) | (lane == h + n_heads) | (lane == h + 2 * n_heads)
    onehot = jnp.where(pick, 1.0, 0.0).astype(BF16)
    krow = lax.broadcasted_iota(jnp.int32, (BQ, BQ), 0)
    qcol = lax.broadcasted_iota(jnp.int32, (BQ, BQ), 1)
    causal = krow <= qcol
    for blk in range(seq // BQ):
        start, end = blk * BQ, (blk + 1) * BQ
        q_aug = jnp.concatenate([q_ref[start:end, :], onehot], axis=1)
        s = _dot_t(kaug_ref[0:end, :], q_aug)
        s_diag = jnp.where(causal, s[start:end, :], -jnp.inf)
        m = jnp.max(s_diag, axis=0, keepdims=True)
        if start:
            s_off = s[0:start, :]
            m = jnp.maximum(m, jnp.max(s_off, axis=0, keepdims=True))
        p = jnp.exp2(s_diag - m)
        denom = jnp.sum(p, axis=0, keepdims=True)
        o_t = jnp.dot(vt_ref[:, start:end], p.astype(BF16), preferred_element_type=F32)
        if start:
            p = jnp.exp2(s_off - m)
            denom = denom + jnp.sum(p, axis=0, keepdims=True)
            o_t = o_t + jnp.dot(vt_ref[:, 0:start], p.astype(BF16), preferred_element_type=F32)
        o_ref[start:end, :] = (o_t / denom).T


def _attention(qkv3, fa3, *, n_heads):
    n_batch, seq, _ = qkv3.shape
    head_blk = lambda part: pl.BlockSpec((None, seq, HEAD_DIM), lambda b, h: (b, 0, part * n_heads + h))
    return pl.pallas_call(
        functools.partial(_attn_kernel, n_heads=n_heads),
        grid=(n_batch, n_heads),
        in_specs=[head_blk(0), head_blk(1), head_blk(2),
                  pl.BlockSpec((None, seq, LANES), lambda b, h: (b, 0, 0))],
        out_specs=head_blk(0),
        out_shape=jax.ShapeDtypeStruct((n_batch, seq, n_heads * HEAD_DIM), F32),
        scratch_shapes=[pltpu.VMEM((seq, HEAD_DIM + LANES), BF16), pltpu.VMEM((HEAD_DIM, seq), BF16)],
        compiler_params=pltpu.CompilerParams(dimension_semantics=("arbitrary", "arbitrary"),
                                             vmem_limit_bytes=VMEM_LIMIT),
        name="fox_attention",
    )(qkv3, qkv3, qkv3, fa3)


def _mix_kernel(u_ref, halo_ref, ya_ref, x_ref, wp_ref, ps_ref, gp_ref, ga_ref, wo_ref, gm_ref, gn_ref,
                h_ref, hn_ref, ext_ref, *, blocks_per_seq, pool_group):
    i = pl.program_id(0)
    bm = u_ref.shape[0]
    pool_width = u_ref.shape[1]
    first = (i % blocks_per_seq) == 0
    ext_ref[0:HALO, :] = jnp.where(first, 0.0, halo_ref[...])
    ext_ref[HALO:HALO + bm, :] = u_ref[...]
    pos = ((i % blocks_per_seq) * bm + lax.broadcasted_iota(jnp.int32, (bm, 1), 0)).astype(F32)

    groups = []
    for gi, w in enumerate(POOL_WINDOWS):
        c0 = gi * pool_group
        cur = ext_ref[HALO:HALO + bm, c0:c0 + pool_group]
        acc = cur
        for j in range(1, w):
            acc = acc + ext_ref[HALO - j:HALO - j + bm, c0:c0 + pool_group]
        count = jnp.minimum(pos + 1.0, float(w))
        y = (acc / count - cur).astype(BF16)
        y = jnp.dot(y, wp_ref[gi].astype(BF16), preferred_element_type=F32)
        groups.append(y * ps_ref[:, c0:c0 + pool_group])
    y_pool = jnp.concatenate(groups, axis=-1)

    n_pool = _rms(y_pool, gp_ref[...]).astype(BF16)
    n_attn = _rms(ya_ref[...], ga_ref[...]).astype(BF16)
    mixed = (jnp.dot(n_pool, wo_ref[0:pool_width, :], preferred_element_type=F32)
             + jnp.dot(n_attn, wo_ref[pool_width:, :], preferred_element_type=F32))
    h1 = x_ref[...] + _rms(mixed, gm_ref[...])
    h_ref[...] = h1
    hn_ref[...] = _rms(h1, gn_ref[...]).astype(BF16)


def _mix(u, y_attn, x2, w_pool, pool_scale, g_pool, g_attn, w_out_bf, g_post, g_next, *, seq):
    rows, d = x2.shape
    pool_width = u.shape[1]
    attn_width = y_attn.shape[1]
    bm = BM_MIX
    blocks_per_seq = seq // bm
    row_blk = lambda w: pl.BlockSpec((bm, w), lambda i: (i, 0))
    halo_blk = pl.BlockSpec((HALO, pool_width), lambda i: (jnp.maximum(i * (bm // HALO) - 1, 0), 0))
    kern = functools.partial(_mix_kernel, blocks_per_seq=blocks_per_seq, pool_group=w_pool.shape[-1])
    return pl.pallas_call(
        kern,
        grid=(rows // bm,),
        in_specs=[row_blk(pool_width), halo_blk, row_blk(attn_width), row_blk(d),
                  _resident(w_pool.shape), _resident((1, pool_width)), _resident((1, pool_width)),
                  _resident((1, attn_width)), _resident(w_out_bf.shape), _resident((1, d)), _resident((1, d))],
        out_specs=[row_blk(d), row_blk(d)],
        out_shape=[jax.ShapeDtypeStruct((rows, d), F32), jax.ShapeDtypeStruct((rows, d), BF16)],
        scratch_shapes=[pltpu.VMEM((HALO + bm, pool_width), F32)],
        compiler_params=pltpu.CompilerParams(dimension_semantics=("arbitrary",),
                                             vmem_limit_bytes=VMEM_LIMIT),
        name="mix_out",
    )(u, u, y_attn, x2, w_pool, pool_scale, g_pool, g_attn, w_out_bf, g_post, g_next)


def _mlp_kernel(hn_ref, wu_ref, wd_ref, o_ref):
    @pl.when(pl.program_id(1) == 0)
    def _():
        o_ref[...] = jnp.zeros_like(o_ref)

    a = jnp.maximum(jnp.dot(hn_ref[...], wu_ref[...].astype(BF16), preferred_element_type=F32), 0.0)
    o_ref[...] += jnp.dot((a * a).astype(BF16), wd_ref[...].astype(BF16), preferred_element_type=F32)


def _mlp(hn, w_up, w_down):
    rows, d = hn.shape
    d_ff = w_up.shape[1]
    bm, bf = BM_MLP, BF_MLP
    return pl.pallas_call(
        _mlp_kernel,
        grid=(rows // bm, d_ff // bf),
        in_specs=[pl.BlockSpec((bm, d), lambda i, j: (i, 0)),
                  pl.BlockSpec((d, bf), lambda i, j: (0, j)),
                  pl.BlockSpec((bf, d), lambda i, j: (j, 0))],
        out_specs=pl.BlockSpec((bm, d), lambda i, j: (i, 0)),
        out_shape=jax.ShapeDtypeStruct((rows, d), F32),
        compiler_params=pltpu.CompilerParams(dimension_semantics=("arbitrary", "arbitrary"),
                                             vmem_limit_bytes=VMEM_LIMIT),
        name="sqrelu_mlp",
    )(hn, w_up, w_down)


def _ple_kernel(h_ref, m_ref, p_ref, gm_ref, gg_ref, wg_ref, wp_ref, gpost_ref, o_ref):
    h = h_ref[...] + _rms(m_ref[...], gm_ref[...])
    hn = _rms(h, gg_ref[...]).astype(BF16)
    gate = jax.nn.sigmoid(jnp.dot(hn, wg_ref[...], preferred_element_type=F32))
    e = jnp.dot(p_ref[...].astype(BF16), wp_ref[...].astype(BF16), preferred_element_type=F32)
    o_ref[...] = h + _rms(gate * e, gpost_ref[...])


def _ple(h1, m, p2, g_post_mlp, g_gate, w_gate_bf, w_ple, g_post):
    rows, d = h1.shape
    bm = BM_PLE
    row_blk = lambda w: pl.BlockSpec((bm, w), lambda i: (i, 0))
    return pl.pallas_call(
        _ple_kernel,
        grid=(rows // bm,),
        in_specs=[row_blk(d), row_blk(d), row_blk(p2.shape[1]),
                  _resident((1, d)), _resident((1, d)), _resident(w_gate_bf.shape), _resident(w_ple.shape),
                  _resident((1, d))],
        out_specs=row_blk(d),
        out_shape=jax.ShapeDtypeStruct((rows, d), F32),
        compiler_params=pltpu.CompilerParams(dimension_semantics=("arbitrary",),
                                             vmem_limit_bytes=VMEM_LIMIT),
        name="ple_gate",
    )(h1, m, p2, g_post_mlp, g_gate, w_gate_bf, w_ple, g_post)


def kernel(x, p, g_pre_mix, w_in, b_f, w_pool, pool_scale, g_pool_out, g_attn_out, w_out, g_post_mix,
           g_pre_mlp, w_up, w_down, g_post_mlp, g_ple_gate, w_gate, w_ple, g_post_ple):
    n_batch, seq, d = x.shape
    depth = w_in.shape[0]
    pool_width = pool_scale.shape[-1]
    attn_width = g_attn_out.shape[-1]
    n_heads = b_f.shape[-1]
    rows = n_batch * seq
    vec = lambda a: a.reshape(1, -1)

    h = x.reshape(rows, d)
    for i in range(depth):
        bias_f = jnp.pad(jnp.tile(b_f[i], FGATE_PIECES), (0, LANES - FGATE_PIECES * n_heads)).reshape(1, LANES)
        u, qkv, f_aug, w_out_bf, w_gate_bf = _inproj(
            h, vec(g_pre_mix[i]), jnp.swapaxes(w_in[i], 0, 1), bias_f, w_out[i], w_gate[i],
            seq=seq, pool_width=pool_width, attn_width=attn_width, n_heads=n_heads)
        y_attn = _attention(qkv.reshape(n_batch, seq, -1), f_aug.reshape(n_batch, seq, LANES), n_heads=n_heads)
        h1, hn = _mix(u, y_attn.reshape(rows, attn_width), h, w_pool[i], vec(pool_scale[i]),
                      vec(g_pool_out[i]), vec(g_attn_out[i]), w_out_bf, vec(g_post_mix[i]),
                      vec(g_pre_mlp[i]), seq=seq)
        m = _mlp(hn, w_up[i], w_down[i])
        h = _ple(h1, m, p[i].reshape(rows, -1), vec(g_post_mlp[i]), vec(g_ple_gate[i]), w_gate_bf,
                 w_ple[i], vec(g_post_ple[i]))
    return h.reshape(n_batch, seq, d)
```

```python
import functools

import jax
import jax.numpy as jnp
from jax import lax
from jax.experimental import pallas as pl
from jax.experimental.pallas import tpu as pltpu

F32 = jnp.float32
BF16 = jnp.bfloat16

EPS = 1e-6
HEAD_DIM = 128
POOL_WINDOWS = (2, 4, 8, 16)
LANES = 128
HALO = 16
LOG2E = 1.4426950408889634
QUERY_SCALE = HEAD_DIM ** -0.5 * LOG2E
FGATE_PIECES = 3

BM_NORM = 512
BM_PROJ = 2048
BN_PROJ = 512
CUMSUM_BLOCK = 256
BM_MIX = 512
ROW_CHUNK = 256
BM_MLP = 1024
BF_MLP = 512
BM_PLE = 512
BQ = 512

VMEM_LIMIT = 56 * 1024 * 1024


def _rms(x, g):
    var = jnp.mean(x * x, axis=-1, keepdims=True)
    return x * lax.rsqrt(var + EPS) * g


def _log_sigmoid(x):
    return jnp.minimum(x, 0.0) - jnp.log1p(jnp.exp(-jnp.abs(x)))


def _resident(shape):
    return pl.BlockSpec(shape, lambda *_: (0,) * len(shape), pipeline_mode=pl.Buffered(1))


def _dot_t(a, b_t):
    return lax.dot_general(a, b_t, (((1,), (1,)), ((), ())), preferred_element_type=F32)


def _split3(v):
    hi = v.astype(BF16)
    rem = v - hi.astype(F32)
    mid = rem.astype(BF16)
    lo = (rem - mid.astype(F32)).astype(BF16)
    return hi, mid, lo


def _prenorm_kernel(x_ref, g_ref, wf_ref, bf_ref, hn_ref, fa_ref, carry_ref, *, blocks_per_seq, n_heads):
    i = pl.program_id(0)
    bm, d = x_ref.shape
    hn = _rms(x_ref[...], g_ref[...]).astype(BF16)
    hn_ref[...] = hn
    wf = wf_ref[...]
    wf_rep = jnp.concatenate([wf] * FGATE_PIECES + [jnp.zeros((LANES - FGATE_PIECES * n_heads, d), F32)],
                             axis=0).astype(BF16)
    logf = _log_sigmoid(_dot_t(hn, wf_rep) + bf_ref[...])
    n = CUMSUM_BLOCK
    row = lax.broadcasted_iota(jnp.int32, (n, n), 0)
    col = lax.broadcasted_iota(jnp.int32, (n, n), 1)
    tri = (col <= row).astype(BF16)
    lane = lax.broadcasted_iota(jnp.int32, (n, LANES), 1)
    carry = jnp.where(i % blocks_per_seq == 0, 0.0, carry_ref[...])
    for sb in range(bm // n):
        hi, mid, lo = _split3(logf[sb * n:(sb + 1) * n, :])
        cs = (jnp.dot(tri, hi, preferred_element_type=F32)
              + jnp.dot(tri, mid, preferred_element_type=F32)
              + jnp.dot(tri, lo, preferred_element_type=F32)) + carry
        carry = cs[n - 1:n, :]
        hi, mid, lo = _split3(cs * (-LOG2E))
        zero = jnp.zeros_like(hi)
        fa_ref[sb * n:(sb + 1) * n, :] = jnp.where(
            lane < n_heads, hi, jnp.where(lane < 2 * n_heads, mid, jnp.where(lane < 3 * n_heads, lo, zero)))
    carry_ref[...] = carry


def _prenorm(x2, g, w_in_t, b_f, *, seq, gate_row0, n_heads):
    rows, d = x2.shape
    bm = BM_NORM
    assert n_heads == 8 and FGATE_PIECES * n_heads <= LANES
    row_blk = lambda w: pl.BlockSpec((bm, w), lambda i: (i, 0))
    kern = functools.partial(_prenorm_kernel, blocks_per_seq=seq // bm, n_heads=n_heads)
    return pl.pallas_call(
        kern,
        grid=(rows // bm,),
        in_specs=[row_blk(d), _resident((1, d)),
                  pl.BlockSpec((n_heads, d), lambda i: (gate_row0 // n_heads, 0)),
                  _resident((1, LANES))],
        out_specs=[row_blk(d), row_blk(LANES)],
        out_shape=[jax.ShapeDtypeStruct((rows, d), BF16), jax.ShapeDtypeStruct((rows, LANES), BF16)],
        scratch_shapes=[pltpu.VMEM((1, LANES), F32)],
        compiler_params=pltpu.CompilerParams(dimension_semantics=("arbitrary",),
                                             vmem_limit_bytes=VMEM_LIMIT),
        name="prenorm_gates",
    )(x2, g, w_in_t, b_f)


def _inproj_kernel(hn_ref, w_ref, wo_ref, wg_ref, u_ref, qkv_ref, wo_bf_ref, wg_bf_ref, *, u_tiles, q_tiles):
    j = pl.program_id(1)

    wo_bf_ref[...] = wo_ref[...].astype(BF16)
    wg_bf_ref[...] = wg_ref[...].astype(BF16)

    @pl.when(j < u_tiles)
    def _():
        u_ref[...] = _dot_t(hn_ref[...], w_ref[...].astype(BF16))

    @pl.when(j >= u_tiles)
    def _():
        c = jnp.where(j < u_tiles + q_tiles, QUERY_SCALE, 1.0)
        qkv_ref[...] = (_dot_t(hn_ref[...], w_ref[...].astype(BF16)) * c).astype(BF16)


def _inproj(hn, w_in_t, w_out, w_gate, *, pool_width, attn_width):
    rows, d = hn.shape
    bm, bn = BM_PROJ, BN_PROJ
    u_tiles = pool_width // bn
    q_tiles = attn_width // bn
    n_tiles = (pool_width + 3 * attn_width) // bn
    n_steps = (rows // bm) * n_tiles
    cast_rows = w_out.shape[0] // n_steps
    cast_blk = pl.BlockSpec((cast_rows, w_out.shape[1]), lambda i, j: (i * n_tiles + j, 0))
    kern = functools.partial(_inproj_kernel, u_tiles=u_tiles, q_tiles=q_tiles)
    return pl.pallas_call(
        kern,
        grid=(rows // bm, n_tiles),
        in_specs=[pl.BlockSpec((bm, d), lambda i, j: (i, 0)),
                  pl.BlockSpec((bn, d), lambda i, j: (j, 0)),
                  cast_blk, cast_blk],
        out_specs=[pl.BlockSpec((bm, bn), lambda i, j: (i, jnp.minimum(j, u_tiles - 1))),
                   pl.BlockSpec((bm, bn), lambda i, j: (i, jnp.maximum(j - u_tiles, 0))),
                   cast_blk, cast_blk],
        out_shape=[jax.ShapeDtypeStruct((rows, pool_width), F32),
                   jax.ShapeDtypeStruct((rows, 3 * attn_width), BF16),
                   jax.ShapeDtypeStruct(w_out.shape, BF16),
                   jax.ShapeDtypeStruct(w_gate.shape, BF16)],
        compiler_params=pltpu.CompilerParams(dimension_semantics=("arbitrary", "arbitrary"),
                                             vmem_limit_bytes=VMEM_LIMIT),
        name="inproj",
    )(hn, w_in_t, w_out, w_gate)


def _attn_kernel(q_ref, k_ref, v_ref, fa_ref, o_ref, kaug_ref, vt_ref, *, n_heads):
    h = pl.program_id(1)
    seq = q_ref.shape[0]
    kaug_ref[:, 0:HEAD_DIM] = k_ref[...]
    kaug_ref[:, HEAD_DIM:] = fa_ref[...]
    vt_ref[...] = v_ref[...].T
    lane = lax.broadcasted_iota(jnp.int32, (BQ, LANES), 1)
    pick = (lane == h) | (lane == h + n_heads) | (lane == h + 2 * n_heads)
    onehot = jnp.where(pick, 1.0, 0.0).astype(BF16)
    krow = lax.broadcasted_iota(jnp.int32, (BQ, BQ), 0)
    qcol = lax.broadcasted_iota(jnp.int32, (BQ, BQ), 1)
    causal = krow <= qcol
    for blk in range(seq // BQ):
        start, end = blk * BQ, (blk + 1) * BQ
        q_aug = jnp.concatenate([q_ref[start:end, :], onehot], axis=1)
        s = _dot_t(kaug_ref[0:end, :], q_aug)
        s_diag = jnp.where(causal, s[start:end, :], -jnp.inf)
        m = jnp.max(s_diag, axis=0, keepdims=True)
        if start:
            s_off = s[0:start, :]
            m = jnp.maximum(m, jnp.max(s_off, axis=0, keepdims=True))
        p = jnp.exp2(s_diag - m)
        denom = jnp.sum(p, axis=0, keepdims=True)
        o_t = jnp.dot(vt_ref[:, start:end], p.astype(BF16), preferred_element_type=F32)
        if start:
            p = jnp.exp2(s_off - m)
            denom = denom + jnp.sum(p, axis=0, keepdims=True)
            o_t = o_t + jnp.dot(vt_ref[:, 0:start], p.astype(BF16), preferred_element_type=F32)
        o_ref[start:end, :] = (o_t / denom).T


def _attention(qkv3, fa3, *, n_heads):
    n_batch, seq, _ = qkv3.shape
    head_blk = lambda part: pl.BlockSpec((None, seq, HEAD_DIM), lambda b, h: (b, 0, part * n_heads + h))
    return pl.pallas_call(
        functools.partial(_attn_kernel, n_heads=n_heads),
        grid=(n_batch, n_heads),
        in_specs=[head_blk(0), head_blk(1), head_blk(2),
                  pl.BlockSpec((None, seq, LANES), lambda b, h: (b, 0, 0))],
        out_specs=head_blk(0),
        out_shape=jax.ShapeDtypeStruct((n_batch, seq, n_heads * HEAD_DIM), F32),
        scratch_shapes=[pltpu.VMEM((seq, HEAD_DIM + LANES), BF16), pltpu.VMEM((HEAD_DIM, seq), BF16)],
        compiler_params=pltpu.CompilerParams(dimension_semantics=("arbitrary", "arbitrary"),
                                             vmem_limit_bytes=VMEM_LIMIT),
        name="fox_attention",
    )(qkv3, qkv3, qkv3, fa3)


def _mix_kernel(u_ref, halo_ref, ya_ref, x_ref, wp_ref, ps_ref, gp_ref, ga_ref, wo_ref, gm_ref, gn_ref,
                h_ref, hn_ref, ext_ref, *, blocks_per_seq, pool_group):
    i = pl.program_id(0)
    bm = u_ref.shape[0]
    pool_width = u_ref.shape[1]
    first = (i % blocks_per_seq) == 0
    ext_ref[0:HALO, :] = jnp.where(first, 0.0, halo_ref[...])
    ext_ref[HALO:HALO + bm, :] = u_ref[...]
    wp = [wp_ref[gi].astype(BF16) for gi in range(len(POOL_WINDOWS))]

    for r0 in range(0, bm, ROW_CHUNK):
        rows = slice(r0, r0 + ROW_CHUNK)
        pos = ((i % blocks_per_seq) * bm + r0
               + lax.broadcasted_iota(jnp.int32, (ROW_CHUNK, 1), 0)).astype(F32)
        groups = []
        for gi, w in enumerate(POOL_WINDOWS):
            c0 = gi * pool_group
            acc = ext_ref[r0:r0 + HALO + ROW_CHUNK, c0:c0 + pool_group]
            cur = acc[HALO:, :]
            span = 1
            while span < w:
                acc = acc + pltpu.roll(acc, span, axis=0)
                span *= 2
            acc = acc[HALO:, :]
            count = jnp.minimum(pos + 1.0, float(w))
            y = (acc / count - cur).astype(BF16)
            y = jnp.dot(y, wp[gi], preferred_element_type=F32)
            groups.append(y * ps_ref[:, c0:c0 + pool_group])
        y_pool = jnp.concatenate(groups, axis=-1)

        n_pool = _rms(y_pool, gp_ref[...]).astype(BF16)
        n_attn = _rms(ya_ref[rows, :], ga_ref[...]).astype(BF16)
        mixed = (jnp.dot(n_pool, wo_ref[0:pool_width, :], preferred_element_type=F32)
                 + jnp.dot(n_attn, wo_ref[pool_width:, :], preferred_element_type=F32))
        h1 = x_ref[rows, :] + _rms(mixed, gm_ref[...])
        h_ref[rows, :] = h1
        hn_ref[rows, :] = _rms(h1, gn_ref[...]).astype(BF16)


def _mix(u, y_attn, x2, w_pool, pool_scale, g_pool, g_attn, w_out_bf, g_post, g_next, *, seq):
    rows, d = x2.shape
    pool_width = u.shape[1]
    attn_width = y_attn.shape[1]
    bm = BM_MIX
    blocks_per_seq = seq // bm
    assert all(w & (w - 1) == 0 and w <= HALO for w in POOL_WINDOWS)
    row_blk = lambda w: pl.BlockSpec((bm, w), lambda i: (i, 0))
    halo_blk = pl.BlockSpec((HALO, pool_width), lambda i: (jnp.maximum(i * (bm // HALO) - 1, 0), 0))
    kern = functools.partial(_mix_kernel, blocks_per_seq=blocks_per_seq, pool_group=w_pool.shape[-1])
    return pl.pallas_call(
        kern,
        grid=(rows // bm,),
        in_specs=[row_blk(pool_width), halo_blk, row_blk(attn_width), row_blk(d),
                  _resident(w_pool.shape), _resident((1, pool_width)), _resident((1, pool_width)),
                  _resident((1, attn_width)), _resident(w_out_bf.shape), _resident((1, d)), _resident((1, d))],
        out_specs=[row_blk(d), row_blk(d)],
        out_shape=[jax.ShapeDtypeStruct((rows, d), F32), jax.ShapeDtypeStruct((rows, d), BF16)],
        scratch_shapes=[pltpu.VMEM((HALO + bm, pool_width), F32)],
        compiler_params=pltpu.CompilerParams(dimension_semantics=("arbitrary",),
                                             vmem_limit_bytes=VMEM_LIMIT),
        name="mix_out",
    )(u, u, y_attn, x2, w_pool, pool_scale, g_pool, g_attn, w_out_bf, g_post, g_next)


def _mlp_kernel(hn_ref, wu_ref, wd_ref, o_ref):
    @pl.when(pl.program_id(1) == 0)
    def _():
        o_ref[...] = jnp.zeros_like(o_ref)

    a = jnp.maximum(jnp.dot(hn_ref[...], wu_ref[...].astype(BF16), preferred_element_type=F32), 0.0)
    o_ref[...] += jnp.dot((a * a).astype(BF16), wd_ref[...].astype(BF16), preferred_element_type=F32)


def _mlp(hn, w_up, w_down):
    rows, d = hn.shape
    d_ff = w_up.shape[1]
    bm, bf = BM_MLP, BF_MLP
    return pl.pallas_call(
        _mlp_kernel,
        grid=(rows // bm, d_ff // bf),
        in_specs=[pl.BlockSpec((bm, d), lambda i, j: (i, 0)),
                  pl.BlockSpec((d, bf), lambda i, j: (0, j)),
                  pl.BlockSpec((bf, d), lambda i, j: (j, 0))],
        out_specs=pl.BlockSpec((bm, d), lambda i, j: (i, 0)),
        out_shape=jax.ShapeDtypeStruct((rows, d), F32),
        compiler_params=pltpu.CompilerParams(dimension_semantics=("arbitrary", "arbitrary"),
                                             vmem_limit_bytes=VMEM_LIMIT),
        name="sqrelu_mlp",
    )(hn, w_up, w_down)


def _ple_kernel(h_ref, m_ref, p_ref, gm_ref, gg_ref, wg_ref, wp_ref, gpost_ref, o_ref):
    h = h_ref[...] + _rms(m_ref[...], gm_ref[...])
    hn = _rms(h, gg_ref[...]).astype(BF16)
    gate = jax.nn.sigmoid(jnp.dot(hn, wg_ref[...], preferred_element_type=F32))
    e = jnp.dot(p_ref[...].astype(BF16), wp_ref[...].astype(BF16), preferred_element_type=F32)
    o_ref[...] = h + _rms(gate * e, gpost_ref[...])


def _ple(h1, m, p2, g_post_mlp, g_gate, w_gate_bf, w_ple, g_post):
    rows, d = h1.shape
    bm = BM_PLE
    row_blk = lambda w: pl.BlockSpec((bm, w), lambda i: (i, 0))
    return pl.pallas_call(
        _ple_kernel,
        grid=(rows // bm,),
        in_specs=[row_blk(d), row_blk(d), row_blk(p2.shape[1]),
                  _resident((1, d)), _resident((1, d)), _resident(w_gate_bf.shape), _resident(w_ple.shape),
                  _resident((1, d))],
        out_specs=row_blk(d),
        out_shape=jax.ShapeDtypeStruct((rows, d), F32),
        compiler_params=pltpu.CompilerParams(dimension_semantics=("arbitrary",),
                                             vmem_limit_bytes=VMEM_LIMIT),
        name="ple_gate",
    )(h1, m, p2, g_post_mlp, g_gate, w_gate_bf, w_ple, g_post)


def kernel(x, p, g_pre_mix, w_in, b_f, w_pool, pool_scale, g_pool_out, g_attn_out, w_out, g_post_mix,
           g_pre_mlp, w_up, w_down, g_post_mlp, g_ple_gate, w_gate, w_ple, g_post_ple):
    n_batch, seq, d = x.shape
    depth = w_in.shape[0]
    pool_width = pool_scale.shape[-1]
    attn_width = g_attn_out.shape[-1]
    n_heads = b_f.shape[-1]
    rows = n_batch * seq
    vec = lambda a: a.reshape(1, -1)

    h = x.reshape(rows, d)
    for i in range(depth):
        bias_f = jnp.pad(jnp.tile(b_f[i], FGATE_PIECES), (0, LANES - FGATE_PIECES * n_heads)).reshape(1, LANES)
        w_in_t = jnp.swapaxes(w_in[i], 0, 1)
        hn0, f_aug = _prenorm(h, vec(g_pre_mix[i]), w_in_t, bias_f, seq=seq,
                              gate_row0=pool_width + 3 * attn_width, n_heads=n_heads)
        u, qkv, w_out_bf, w_gate_bf = _inproj(hn0, w_in_t, w_out[i], w_gate[i],
                                              pool_width=pool_width, attn_width=attn_width)
        y_attn = _attention(qkv.reshape(n_batch, seq, -1), f_aug.reshape(n_batch, seq, LANES), n_heads=n_heads)
        h1, hn = _mix(u, y_attn.reshape(rows, attn_width), h, w_pool[i], vec(pool_scale[i]),
                      vec(g_pool_out[i]), vec(g_attn_out[i]), w_out_bf, vec(g_post_mix[i]),
                      vec(g_pre_mlp[i]), seq=seq)
        m = _mlp(hn, w_up[i], w_down[i])
        h = _ple(h1, m, p[i].reshape(rows, -1), vec(g_post_mlp[i]), vec(g_ple_gate[i]), w_gate_bf,
                 w_ple[i], vec(g_post_ple[i]))
    return h.reshape(n_batch, seq, d)
```

```python
import functools

import jax
import jax.numpy as jnp
from jax import lax
from jax.experimental import pallas as pl
from jax.experimental.pallas import tpu as pltpu

F32 = jnp.float32
BF16 = jnp.bfloat16

EPS = 1e-6
HEAD_DIM = 128
POOL_WINDOWS = (2, 4, 8, 16)
LANES = 128
HALO = 16
LOG2E = 1.4426950408889634
QUERY_SCALE = HEAD_DIM ** -0.5 * LOG2E
FGATE_PIECES = 3

BM_NORM = 512
BM_PROJ = 2048
BN_PROJ = 512
CUMSUM_BLOCK = 256
BM_MIX = 512
ROW_CHUNK = 256
BM_MLP = 1024
BF_MLP = 512
BM_PLE = 512
BQ = 512

VMEM_LIMIT = 56 * 1024 * 1024


def _rms(x, g):
    var = jnp.mean(x * x, axis=-1, keepdims=True)
    return x * lax.rsqrt(var + EPS) * g


def _log_sigmoid(x):
    return jnp.minimum(x, 0.0) - jnp.log1p(jnp.exp(-jnp.abs(x)))


def _resident(shape):
    return pl.BlockSpec(shape, lambda *_: (0,) * len(shape), pipeline_mode=pl.Buffered(1))


def _dot_t(a, b_t):
    return lax.dot_general(a, b_t, (((1,), (1,)), ((), ())), preferred_element_type=F32)


def _split3(v):
    hi = v.astype(BF16)
    rem = v - hi.astype(F32)
    mid = rem.astype(BF16)
    lo = (rem - mid.astype(F32)).astype(BF16)
    return hi, mid, lo


def _prenorm_kernel(x_ref, g_ref, wf_ref, bf_ref, hn_ref, fa_ref, carry_ref, *, blocks_per_seq, n_heads):
    i = pl.program_id(0)
    bm, d = x_ref.shape
    hn = _rms(x_ref[...], g_ref[...]).astype(BF16)
    hn_ref[...] = hn
    wf = wf_ref[...]
    wf_rep = jnp.concatenate([wf] * FGATE_PIECES + [jnp.zeros((LANES - FGATE_PIECES * n_heads, d), F32)],
                             axis=0).astype(BF16)
    logf = _log_sigmoid(_dot_t(hn, wf_rep) + bf_ref[...])
    n = CUMSUM_BLOCK
    row = lax.broadcasted_iota(jnp.int32, (n, n), 0)
    col = lax.broadcasted_iota(jnp.int32, (n, n), 1)
    tri = (col <= row).astype(BF16)
    lane = lax.broadcasted_iota(jnp.int32, (n, LANES), 1)
    carry = jnp.where(i % blocks_per_seq == 0, 0.0, carry_ref[...])
    for sb in range(bm // n):
        hi, mid, lo = _split3(logf[sb * n:(sb + 1) * n, :])
        cs = (jnp.dot(tri, hi, preferred_element_type=F32)
              + jnp.dot(tri, mid, preferred_element_type=F32)
              + jnp.dot(tri, lo, preferred_element_type=F32)) + carry
        carry = cs[n - 1:n, :]
        hi, mid, lo = _split3(cs * (-LOG2E))
        zero = jnp.zeros_like(hi)
        fa_ref[sb * n:(sb + 1) * n, :] = jnp.where(
            lane < n_heads, hi, jnp.where(lane < 2 * n_heads, mid, jnp.where(lane < 3 * n_heads, lo, zero)))
    carry_ref[...] = carry


def _prenorm(x2, g, w_in_t, b_f, *, seq, gate_row0, n_heads):
    rows, d = x2.shape
    bm = BM_NORM
    assert n_heads == 8 and FGATE_PIECES * n_heads <= LANES
    row_blk = lambda w: pl.BlockSpec((bm, w), lambda i: (i, 0))
    kern = functools.partial(_prenorm_kernel, blocks_per_seq=seq // bm, n_heads=n_heads)
    return pl.pallas_call(
        kern,
        grid=(rows // bm,),
        in_specs=[row_blk(d), _resident((1, d)),
                  pl.BlockSpec((n_heads, d), lambda i: (gate_row0 // n_heads, 0)),
                  _resident((1, LANES))],
        out_specs=[row_blk(d), row_blk(LANES)],
        out_shape=[jax.ShapeDtypeStruct((rows, d), BF16), jax.ShapeDtypeStruct((rows, LANES), BF16)],
        scratch_shapes=[pltpu.VMEM((1, LANES), F32)],
        compiler_params=pltpu.CompilerParams(dimension_semantics=("arbitrary",),
                                             vmem_limit_bytes=VMEM_LIMIT),
        name="prenorm_gates",
    )(x2, g, w_in_t, b_f)


def _inproj_kernel(hn_ref, w_ref, wo_ref, wg_ref, u_ref, qk_ref, vt_ref, wo_bf_ref, wg_bf_ref,
                   *, u_tiles, q_tiles):
    j = pl.program_id(1)
    v_start = u_tiles + 2 * q_tiles

    wo_bf_ref[...] = wo_ref[...].astype(BF16)
    wg_bf_ref[...] = wg_ref[...].astype(BF16)

    @pl.when(j < u_tiles)
    def _():
        u_ref[...] = _dot_t(hn_ref[...], w_ref[...].astype(BF16))

    @pl.when((j >= u_tiles) & (j < v_start))
    def _():
        c = jnp.where(j < u_tiles + q_tiles, QUERY_SCALE, 1.0)
        qk_ref[...] = (_dot_t(hn_ref[...], w_ref[...].astype(BF16)) * c).astype(BF16)

    @pl.when(j >= v_start)
    def _():
        vt_ref[...] = _dot_t(w_ref[...].astype(BF16), hn_ref[...]).astype(BF16)


def _inproj(hn, w_in_t, w_out, w_gate, *, pool_width, attn_width):
    rows, d = hn.shape
    bm, bn = BM_PROJ, BN_PROJ
    u_tiles = pool_width // bn
    q_tiles = attn_width // bn
    n_tiles = (pool_width + 3 * attn_width) // bn
    n_steps = (rows // bm) * n_tiles
    cast_rows = w_out.shape[0] // n_steps
    cast_blk = pl.BlockSpec((cast_rows, w_out.shape[1]), lambda i, j: (i * n_tiles + j, 0))
    kern = functools.partial(_inproj_kernel, u_tiles=u_tiles, q_tiles=q_tiles)
    return pl.pallas_call(
        kern,
        grid=(rows // bm, n_tiles),
        in_specs=[pl.BlockSpec((bm, d), lambda i, j: (i, 0)),
                  pl.BlockSpec((bn, d), lambda i, j: (j, 0)),
                  cast_blk, cast_blk],
        out_specs=[pl.BlockSpec((bm, bn), lambda i, j: (i, jnp.minimum(j, u_tiles - 1))),
                   pl.BlockSpec((bm, bn), lambda i, j: (i, jnp.clip(j - u_tiles, 0, 2 * q_tiles - 1))),
                   pl.BlockSpec((bn, bm), lambda i, j: (jnp.maximum(j - u_tiles - 2 * q_tiles, 0), i)),
                   cast_blk, cast_blk],
        out_shape=[jax.ShapeDtypeStruct((rows, pool_width), F32),
                   jax.ShapeDtypeStruct((rows, 2 * attn_width), BF16),
                   jax.ShapeDtypeStruct((attn_width, rows), BF16),
                   jax.ShapeDtypeStruct(w_out.shape, BF16),
                   jax.ShapeDtypeStruct(w_gate.shape, BF16)],
        compiler_params=pltpu.CompilerParams(dimension_semantics=("arbitrary", "arbitrary"),
                                             vmem_limit_bytes=VMEM_LIMIT),
        name="inproj",
    )(hn, w_in_t, w_out, w_gate)


def _attn_kernel(q_ref, k_ref, vt_ref, fa_ref, wu_ref, o_ref, wu_bf_ref, kaug_ref, *s_refs, n_heads):
    h = pl.program_id(1)
    seq = q_ref.shape[0]
    wu_bf_ref[...] = wu_ref[...].astype(BF16)
    kaug_ref[:, 0:HEAD_DIM] = k_ref[...]
    kaug_ref[:, HEAD_DIM:] = fa_ref[...]
    lane = lax.broadcasted_iota(jnp.int32, (BQ, LANES), 1)
    pick = (lane == h) | (lane == h + n_heads) | (lane == h + 2 * n_heads)
    onehot = jnp.where(pick, 1.0, 0.0).astype(BF16)
    krow = lax.broadcasted_iota(jnp.int32, (BQ, BQ), 0)
    qcol = lax.broadcasted_iota(jnp.int32, (BQ, BQ), 1)
    causal = krow <= qcol
    n_blk = seq // BQ
    maxes = []
    for blk in range(n_blk):
        start, end = blk * BQ, (blk + 1) * BQ
        s_ref = s_refs[blk]
        q_aug = jnp.concatenate([q_ref[start:end, :], onehot], axis=1)
        s = _dot_t(kaug_ref[0:end, :], q_aug)
        s_diag = jnp.where(causal, s[start:end, :], -jnp.inf)
        s_ref[start:end, :] = s_diag
        m = jnp.max(s_diag, axis=0, keepdims=True)
        if start:
            s_ref[0:start, :] = s[0:start, :]
            m = jnp.maximum(m, jnp.max(s[0:start, :], axis=0, keepdims=True))
        maxes.append(m)
    for blk in range(n_blk):
        s_ref, m = s_refs[blk], maxes[blk]
        denom = jnp.zeros((1, BQ), F32)
        o_t = jnp.zeros((HEAD_DIM, BQ), F32)
        for c in range(blk + 1):
            keys = slice(c * BQ, (c + 1) * BQ)
            p = jnp.exp2(s_ref[keys, :] - m)
            denom = denom + jnp.sum(p, axis=0, keepdims=True)
            o_t = o_t + jnp.dot(vt_ref[:, keys], p.astype(BF16), preferred_element_type=F32)
        o_ref[blk * BQ:(blk + 1) * BQ, :] = (o_t / denom).T


def _attention(qk3, v_t, fa3, w_up, *, n_heads):
    n_batch, seq, _ = qk3.shape
    d, d_ff = w_up.shape
    n_tiles = d_ff // BF_MLP
    parts = (n_batch * n_heads) // n_tiles
    part_rows = d // parts
    head_blk = lambda part: pl.BlockSpec((None, seq, HEAD_DIM), lambda b, h: (b, 0, part * n_heads + h))
    step = lambda b, h: b * n_heads + h
    return pl.pallas_call(
        functools.partial(_attn_kernel, n_heads=n_heads),
        grid=(n_batch, n_heads),
        in_specs=[head_blk(0), head_blk(1),
                  pl.BlockSpec((HEAD_DIM, seq), lambda b, h: (h, b)),
                  pl.BlockSpec((None, seq, LANES), lambda b, h: (b, 0, 0)),
                  pl.BlockSpec((part_rows, BF_MLP), lambda b, h: (step(b, h) % parts, step(b, h) // parts))],
        out_specs=[head_blk(0),
                   pl.BlockSpec((None, part_rows, BF_MLP), lambda b, h: (step(b, h) // parts, step(b, h) % parts, 0))],
        out_shape=[jax.ShapeDtypeStruct((n_batch, seq, n_heads * HEAD_DIM), F32),
                   jax.ShapeDtypeStruct((n_tiles, d, BF_MLP), BF16)],
        scratch_shapes=[pltpu.VMEM((seq, HEAD_DIM + LANES), BF16)]
                       + [pltpu.VMEM(((blk + 1) * BQ, BQ), F32) for blk in range(seq // BQ)],
        compiler_params=pltpu.CompilerParams(dimension_semantics=("arbitrary", "arbitrary"),
                                             vmem_limit_bytes=VMEM_LIMIT),
        name="fox_attention",
    )(qk3, qk3, v_t, fa3, w_up)


def _mix_kernel(u_ref, halo_ref, ya_ref, x_ref, wp_ref, ps_ref, gp_ref, ga_ref, wo_ref, gm_ref, gn_ref,
                h_ref, hn_ref, ext_ref, *, blocks_per_seq, pool_group):
    i = pl.program_id(0)
    bm = u_ref.shape[0]
    pool_width = u_ref.shape[1]
    first = (i % blocks_per_seq) == 0
    ext_ref[0:HALO, :] = jnp.where(first, 0.0, halo_ref[...])
    ext_ref[HALO:HALO + bm, :] = u_ref[...]
    wp = [wp_ref[gi].astype(BF16) for gi in range(len(POOL_WINDOWS))]

    for r0 in range(0, bm, ROW_CHUNK):
        rows = slice(r0, r0 + ROW_CHUNK)
        pos = ((i % blocks_per_seq) * bm + r0
               + lax.broadcasted_iota(jnp.int32, (ROW_CHUNK, 1), 0)).astype(F32)
        groups = []
        for gi, w in enumerate(POOL_WINDOWS):
            c0 = gi * pool_group
            acc = ext_ref[r0:r0 + HALO + ROW_CHUNK, c0:c0 + pool_group]
            cur = acc[HALO:, :]
            span = 1
            while span < w:
                acc = acc + pltpu.roll(acc, span, axis=0)
                span *= 2
            acc = acc[HALO:, :]
            count = jnp.minimum(pos + 1.0, float(w))
            y = (acc / count - cur).astype(BF16)
            y = jnp.dot(y, wp[gi], preferred_element_type=F32)
            groups.append(y * ps_ref[:, c0:c0 + pool_group])
        y_pool = jnp.concatenate(groups, axis=-1)

        n_pool = _rms(y_pool, gp_ref[...]).astype(BF16)
        n_attn = _rms(ya_ref[rows, :], ga_ref[...]).astype(BF16)
        mixed = (jnp.dot(n_pool, wo_ref[0:pool_width, :], preferred_element_type=F32)
                 + jnp.dot(n_attn, wo_ref[pool_width:, :], preferred_element_type=F32))
        h1 = x_ref[rows, :] + _rms(mixed, gm_ref[...])
        h_ref[rows, :] = h1
        hn_ref[rows, :] = _rms(h1, gn_ref[...]).astype(BF16)


def _mix(u, y_attn, x2, w_pool, pool_scale, g_pool, g_attn, w_out_bf, g_post, g_next, *, seq):
    rows, d = x2.shape
    pool_width = u.shape[1]
    attn_width = y_attn.shape[1]
    bm = BM_MIX
    blocks_per_seq = seq // bm
    assert all(w & (w - 1) == 0 and w <= HALO for w in POOL_WINDOWS)
    row_blk = lambda w: pl.BlockSpec((bm, w), lambda i: (i, 0))
    halo_blk = pl.BlockSpec((HALO, pool_width), lambda i: (jnp.maximum(i * (bm // HALO) - 1, 0), 0))
    kern = functools.partial(_mix_kernel, blocks_per_seq=blocks_per_seq, pool_group=w_pool.shape[-1])
    return pl.pallas_call(
        kern,
        grid=(rows // bm,),
        in_specs=[row_blk(pool_width), halo_blk, row_blk(attn_width), row_blk(d),
                  _resident(w_pool.shape), _resident((1, pool_width)), _resident((1, pool_width)),
                  _resident((1, attn_width)), _resident(w_out_bf.shape), _resident((1, d)), _resident((1, d))],
        out_specs=[row_blk(d), row_blk(d)],
        out_shape=[jax.ShapeDtypeStruct((rows, d), F32), jax.ShapeDtypeStruct((rows, d), BF16)],
        scratch_shapes=[pltpu.VMEM((HALO + bm, pool_width), F32)],
        compiler_params=pltpu.CompilerParams(dimension_semantics=("arbitrary",),
                                             vmem_limit_bytes=VMEM_LIMIT),
        name="mix_out",
    )(u, u, y_attn, x2, w_pool, pool_scale, g_pool, g_attn, w_out_bf, g_post, g_next)


def _mlp_kernel(hn_ref, wu_ref, wd_ref, o_ref):
    @pl.when(pl.program_id(1) == 0)
    def _():
        o_ref[...] = jnp.zeros_like(o_ref)

    a = jnp.maximum(jnp.dot(hn_ref[...], wu_ref[...], preferred_element_type=F32), 0.0)
    o_ref[...] += jnp.dot((a * a).astype(BF16), wd_ref[...].astype(BF16), preferred_element_type=F32)


def _mlp(hn, w_up_tiles, w_down):
    rows, d = hn.shape
    n_tiles, _, bf = w_up_tiles.shape
    bm = BM_MLP
    return pl.pallas_call(
        _mlp_kernel,
        grid=(rows // bm, n_tiles),
        in_specs=[pl.BlockSpec((bm, d), lambda i, j: (i, 0)),
                  pl.BlockSpec((None, d, bf), lambda i, j: (j, 0, 0)),
                  pl.BlockSpec((bf, d), lambda i, j: (j, 0))],
        out_specs=pl.BlockSpec((bm, d), lambda i, j: (i, 0)),
        out_shape=jax.ShapeDtypeStruct((rows, d), F32),
        compiler_params=pltpu.CompilerParams(dimension_semantics=("arbitrary", "arbitrary"),
                                             vmem_limit_bytes=VMEM_LIMIT),
        name="sqrelu_mlp",
    )(hn, w_up_tiles, w_down)


def _ple_kernel(h_ref, m_ref, p_ref, gm_ref, gg_ref, wg_ref, wp_ref, gpost_ref, o_ref):
    h = h_ref[...] + _rms(m_ref[...], gm_ref[...])
    hn = _rms(h, gg_ref[...]).astype(BF16)
    gate = jax.nn.sigmoid(jnp.dot(hn, wg_ref[...], preferred_element_type=F32))
    e = jnp.dot(p_ref[...].astype(BF16), wp_ref[...].astype(BF16), preferred_element_type=F32)
    o_ref[...] = h + _rms(gate * e, gpost_ref[...])


def _ple(h1, m, p2, g_post_mlp, g_gate, w_gate_bf, w_ple, g_post):
    rows, d = h1.shape
    bm = BM_PLE
    row_blk = lambda w: pl.BlockSpec((bm, w), lambda i: (i, 0))
    return pl.pallas_call(
        _ple_kernel,
        grid=(rows // bm,),
        in_specs=[row_blk(d), row_blk(d), row_blk(p2.shape[1]),
                  _resident((1, d)), _resident((1, d)), _resident(w_gate_bf.shape), _resident(w_ple.shape),
                  _resident((1, d))],
        out_specs=row_blk(d),
        out_shape=jax.ShapeDtypeStruct((rows, d), F32),
        compiler_params=pltpu.CompilerParams(dimension_semantics=("arbitrary",),
                                             vmem_limit_bytes=VMEM_LIMIT),
        name="ple_gate",
    )(h1, m, p2, g_post_mlp, g_gate, w_gate_bf, w_ple, g_post)


def kernel(x, p, g_pre_mix, w_in, b_f, w_pool, pool_scale, g_pool_out, g_attn_out, w_out, g_post_mix,
           g_pre_mlp, w_up, w_down, g_post_mlp, g_ple_gate, w_gate, w_ple, g_post_ple):
    n_batch, seq, d = x.shape
    depth = w_in.shape[0]
    pool_width = pool_scale.shape[-1]
    attn_width = g_attn_out.shape[-1]
    n_heads = b_f.shape[-1]
    rows = n_batch * seq
    vec = lambda a: a.reshape(1, -1)

    h = x.reshape(rows, d)
    for i in range(depth):
        bias_f = jnp.pad(jnp.tile(b_f[i], FGATE_PIECES), (0, LANES - FGATE_PIECES * n_heads)).reshape(1, LANES)
        w_in_t = jnp.swapaxes(w_in[i], 0, 1)
        hn0, f_aug = _prenorm(h, vec(g_pre_mix[i]), w_in_t, bias_f, seq=seq,
                              gate_row0=pool_width + 3 * attn_width, n_heads=n_heads)
        u, qk, v_t, w_out_bf, w_gate_bf = _inproj(hn0, w_in_t, w_out[i], w_gate[i],
                                                  pool_width=pool_width, attn_width=attn_width)
        y_attn, w_up_tiles = _attention(qk.reshape(n_batch, seq, -1), v_t, f_aug.reshape(n_batch, seq, LANES),
                                        w_up[i], n_heads=n_heads)
        h1, hn = _mix(u, y_attn.reshape(rows, attn_width), h, w_pool[i], vec(pool_scale[i]),
                      vec(g_pool_out[i]), vec(g_attn_out[i]), w_out_bf, vec(g_post_mix[i]),
                      vec(g_pre_mlp[i]), seq=seq)
        m = _mlp(hn, w_up_tiles, w_down[i])
        h = _ple(h1, m, p[i].reshape(rows, -1), vec(g_post_mlp[i]), vec(g_ple_gate[i]), w_gate_bf,
                 w_ple[i], vec(g_post_ple[i]))
    return h.reshape(n_batch, seq, d)
```

```python
import functools

import jax
import jax.numpy as jnp
from jax import lax
from jax.experimental import pallas as pl
from jax.experimental.pallas import tpu as pltpu

F32 = jnp.float32
BF16 = jnp.bfloat16

EPS = 1e-6
HEAD_DIM = 128
POOL_WINDOWS = (2, 4, 8, 16)
LANES = 128
SUBLANES = 8
HALO = 16
LOG2E = 1.4426950408889634
QUERY_SCALE = HEAD_DIM ** -0.5 * LOG2E
FGATE_PIECES = 3

BM_NORM = 2048
BM_PROJ = 2048
BN_PROJ = 512
CUMSUM_BLOCK = 256
BM_MIX = 512
ROW_CHUNK = 512
BM_MLP = 1024
BF_MLP = 1024
MLP_SUB = 512
BM_PLE = 512
BQ = 512
HEADS_PER_STEP = 2

VMEM_LIMIT = 56 * 1024 * 1024


def _rms(x, g):
    var = jnp.mean(x * x, axis=-1, keepdims=True)
    return x * lax.rsqrt(var + EPS) * g


def _log_sigmoid(x):
    return jnp.minimum(x, 0.0) - jnp.log1p(jnp.exp(-jnp.abs(x)))


def _resident(shape):
    return pl.BlockSpec(shape, lambda *_: (0,) * len(shape), pipeline_mode=pl.Buffered(1))


def _dot_t(a, b_t):
    return lax.dot_general(a, b_t, (((1,), (1,)), ((), ())), preferred_element_type=F32)


def _split3(v):
    hi = v.astype(BF16)
    rem = v - hi.astype(F32)
    mid = rem.astype(BF16)
    lo = (rem - mid.astype(F32)).astype(BF16)
    return hi, mid, lo


def _prenorm_kernel(x_ref, g_ref, wf_ref, bf_ref, hn_ref, fa_ref, carry_ref, *, blocks_per_seq, n_heads):
    i = pl.program_id(0)
    bm, d = x_ref.shape
    hn = _rms(x_ref[...], g_ref[...]).astype(BF16)
    hn_ref[...] = hn
    wf = wf_ref[...]
    wf_rep = jnp.concatenate([wf] * FGATE_PIECES + [jnp.zeros((LANES - FGATE_PIECES * n_heads, d), F32)],
                             axis=0).astype(BF16)
    logf = _log_sigmoid(_dot_t(hn, wf_rep) + bf_ref[...])
    n = CUMSUM_BLOCK
    row = lax.broadcasted_iota(jnp.int32, (n, n), 0)
    col = lax.broadcasted_iota(jnp.int32, (n, n), 1)
    tri = (col <= row).astype(BF16)
    lane = lax.broadcasted_iota(jnp.int32, (n, LANES), 1)
    carry = jnp.where(i % blocks_per_seq == 0, 0.0, carry_ref[...])
    for sb in range(bm // n):
        hi, mid, lo = _split3(logf[sb * n:(sb + 1) * n, :])
        cs = (jnp.dot(tri, hi, preferred_element_type=F32)
              + jnp.dot(tri, mid, preferred_element_type=F32)
              + jnp.dot(tri, lo, preferred_element_type=F32)) + carry
        carry = cs[n - 1:n, :]
        hi, mid, lo = _split3(cs * (-LOG2E))
        zero = jnp.zeros_like(hi)
        fa_ref[sb * n:(sb + 1) * n, :] = jnp.where(
            lane < n_heads, hi, jnp.where(lane < 2 * n_heads, mid, jnp.where(lane < 3 * n_heads, lo, zero)))
    carry_ref[...] = carry


def _prenorm(x2, g, w_in_t, b_f, *, seq, gate_row0, n_heads):
    rows, d = x2.shape
    bm = BM_NORM
    assert n_heads == SUBLANES and FGATE_PIECES * n_heads <= LANES
    row_blk = lambda w: pl.BlockSpec((bm, w), lambda i: (i, 0))
    kern = functools.partial(_prenorm_kernel, blocks_per_seq=seq // bm, n_heads=n_heads)
    return pl.pallas_call(
        kern,
        grid=(rows // bm,),
        in_specs=[row_blk(d), _resident((1, d)),
                  pl.BlockSpec((n_heads, d), lambda i: (gate_row0 // n_heads, 0)),
                  _resident((1, LANES))],
        out_specs=[row_blk(d), row_blk(LANES)],
        out_shape=[jax.ShapeDtypeStruct((rows, d), BF16), jax.ShapeDtypeStruct((rows, LANES), BF16)],
        scratch_shapes=[pltpu.VMEM((1, LANES), F32)],
        compiler_params=pltpu.CompilerParams(dimension_semantics=("arbitrary",),
                                             vmem_limit_bytes=VMEM_LIMIT),
        name="prenorm_gates",
    )(x2, g, w_in_t, b_f)


def _inproj_kernel(hn_ref, w_ref, wo_ref, wg_ref, u_ref, qk_ref, vt_ref, wo_bf_ref, wg_bf_ref,
                   *, u_tiles, q_tiles):
    j = pl.program_id(1)
    v_start = u_tiles + 2 * q_tiles

    wo_bf_ref[...] = wo_ref[...].astype(BF16)
    wg_bf_ref[...] = wg_ref[...].astype(BF16)

    @pl.when(j < u_tiles)
    def _():
        u_ref[...] = _dot_t(hn_ref[...], w_ref[...].astype(BF16))

    @pl.when((j >= u_tiles) & (j < v_start))
    def _():
        c = jnp.where(j < u_tiles + q_tiles, QUERY_SCALE, 1.0)
        z = (_dot_t(hn_ref[...], w_ref[...].astype(BF16)) * c).astype(BF16)
        for hh in range(qk_ref.shape[0]):
            qk_ref[hh] = z[:, hh * HEAD_DIM:(hh + 1) * HEAD_DIM]

    @pl.when(j >= v_start)
    def _():
        vt_ref[...] = _dot_t(w_ref[...].astype(BF16), hn_ref[...]).astype(BF16)


def _inproj(hn, w_in_t, w_out, w_gate, *, pool_width, attn_width):
    rows, d = hn.shape
    bm, bn = BM_PROJ, BN_PROJ
    u_tiles = pool_width // bn
    q_tiles = attn_width // bn
    heads_per_tile = bn // HEAD_DIM
    n_tiles = (pool_width + 3 * attn_width) // bn
    n_steps = (rows // bm) * n_tiles
    cast_rows = w_out.shape[0] // n_steps
    cast_blk = pl.BlockSpec((cast_rows, w_out.shape[1]), lambda i, j: (i * n_tiles + j, 0))
    kern = functools.partial(_inproj_kernel, u_tiles=u_tiles, q_tiles=q_tiles)
    return pl.pallas_call(
        kern,
        grid=(rows // bm, n_tiles),
        in_specs=[pl.BlockSpec((bm, d), lambda i, j: (i, 0)),
                  pl.BlockSpec((bn, d), lambda i, j: (j, 0)),
                  cast_blk, cast_blk],
        out_specs=[pl.BlockSpec((None, bm, bn), lambda i, j: (jnp.minimum(j, u_tiles - 1), i, 0)),
                   pl.BlockSpec((heads_per_tile, bm, HEAD_DIM),
                                lambda i, j: (jnp.clip(j - u_tiles, 0, 2 * q_tiles - 1), i, 0)),
                   pl.BlockSpec((bn, bm), lambda i, j: (jnp.maximum(j - u_tiles - 2 * q_tiles, 0), i)),
                   cast_blk, cast_blk],
        out_shape=[jax.ShapeDtypeStruct((u_tiles, rows, bn), F32),
                   jax.ShapeDtypeStruct((2 * attn_width // HEAD_DIM, rows, HEAD_DIM), BF16),
                   jax.ShapeDtypeStruct((attn_width, rows), BF16),
                   jax.ShapeDtypeStruct(w_out.shape, BF16),
                   jax.ShapeDtypeStruct(w_gate.shape, BF16)],
        compiler_params=pltpu.CompilerParams(dimension_semantics=("arbitrary", "arbitrary"),
                                             vmem_limit_bytes=VMEM_LIMIT),
        name="inproj",
    )(hn, w_in_t, w_out, w_gate)


def _attn_kernel(q_ref, k_ref, vt_ref, fa_ref, wu_ref, o_ref, wu_bf_ref, kaug_ref, *s_refs, n_heads):
    heads_here, seq, _ = q_ref.shape
    n_blk = seq // BQ
    wu_bf_ref[...] = wu_ref[...].astype(BF16)
    lane = lax.broadcasted_iota(jnp.int32, (BQ, LANES), 1)
    krow = lax.broadcasted_iota(jnp.int32, (BQ, BQ), 0)
    qcol = lax.broadcasted_iota(jnp.int32, (BQ, BQ), 1)
    causal = krow <= qcol
    maxes = []
    for hh in range(heads_here):
        h = pl.program_id(1) * heads_here + hh
        kaug_ref[hh, :, 0:HEAD_DIM] = k_ref[hh]
        kaug_ref[hh, :, HEAD_DIM:] = fa_ref[...]
        pick = (lane == h) | (lane == h + n_heads) | (lane == h + 2 * n_heads)
        onehot = jnp.where(pick, 1.0, 0.0).astype(BF16)
        for blk in range(n_blk):
            start, end = blk * BQ, (blk + 1) * BQ
            s_ref = s_refs[hh * n_blk + blk]
            q_aug = jnp.concatenate([q_ref[hh, start:end, :], onehot], axis=1)
            s = _dot_t(kaug_ref[hh, 0:end, :], q_aug)
            s_diag = jnp.where(causal, s[start:end, :], -jnp.inf)
            s_ref[start:end, :] = s_diag
            m = jnp.max(s_diag, axis=0, keepdims=True)
            if start:
                s_ref[0:start, :] = s[0:start, :]
                m = jnp.maximum(m, jnp.max(s[0:start, :], axis=0, keepdims=True))
            maxes.append(m)
    for hh in range(heads_here):
        for blk in range(n_blk):
            s_ref, m = s_refs[hh * n_blk + blk], maxes[hh * n_blk + blk]
            denom = jnp.zeros((1, BQ), F32)
            o_t = jnp.zeros((HEAD_DIM, BQ), F32)
            for c in range(blk + 1):
                keys = slice(c * BQ, (c + 1) * BQ)
                p = jnp.exp2(s_ref[keys, :] - m)
                denom = denom + jnp.sum(p, axis=0, keepdims=True)
                o_t = o_t + jnp.dot(vt_ref[hh * HEAD_DIM:(hh + 1) * HEAD_DIM, keys], p.astype(BF16),
                                    preferred_element_type=F32)
            o_ref[hh, blk * BQ:(blk + 1) * BQ, :] = (o_t / denom).T


def _attention(qk_heads, v_t, f_aug, w_up, *, n_batch, n_heads):
    _, rows, _ = qk_heads.shape
    seq = rows // n_batch
    d, d_ff = w_up.shape
    n_tiles = d_ff // BF_MLP
    hps = HEADS_PER_STEP
    groups = n_heads // hps
    parts = (n_batch * groups) // n_tiles
    part_rows = d // parts
    head_blk = lambda part: pl.BlockSpec((hps, seq, HEAD_DIM), lambda b, g: (part * groups + g, b, 0))
    step = lambda b, g: b * groups + g
    return pl.pallas_call(
        functools.partial(_attn_kernel, n_heads=n_heads),
        grid=(n_batch, groups),
        in_specs=[head_blk(0), head_blk(1),
                  pl.BlockSpec((hps * HEAD_DIM, seq), lambda b, g: (g, b)),
                  pl.BlockSpec((seq, LANES), lambda b, g: (b, 0)),
                  pl.BlockSpec((part_rows, BF_MLP), lambda b, g: (step(b, g) % parts, step(b, g) // parts))],
        out_specs=[head_blk(0),
                   pl.BlockSpec((None, part_rows, BF_MLP), lambda b, g: (step(b, g) // parts, step(b, g) % parts, 0))],
        out_shape=[jax.ShapeDtypeStruct((n_heads, rows, HEAD_DIM), F32),
                   jax.ShapeDtypeStruct((n_tiles, d, BF_MLP), BF16)],
        scratch_shapes=[pltpu.VMEM((hps, seq, HEAD_DIM + LANES), BF16)]
                       + [pltpu.VMEM(((blk + 1) * BQ, BQ), F32) for _ in range(hps) for blk in range(seq // BQ)],
        compiler_params=pltpu.CompilerParams(dimension_semantics=("arbitrary", "arbitrary"),
                                             vmem_limit_bytes=VMEM_LIMIT),
        name="fox_attention",
    )(qk_heads, qk_heads, v_t, f_aug, w_up)


def _mix_kernel(u_ref, halo_ref, ya_ref, x_ref, wp_ref, ps_ref, gp_ref, ga_ref, wo_ref, gm_ref, gn_ref,
                h_ref, hn_ref, ext_ref, *, blocks_per_seq, pool_group):
    i = pl.program_id(0)
    u_tiles, bm, bn = u_ref.shape
    pool_width = u_tiles * bn
    first = (i % blocks_per_seq) == 0
    for t in range(u_tiles):
        ext_ref[0:HALO, t * bn:(t + 1) * bn] = jnp.where(first, 0.0, halo_ref[t])
        ext_ref[HALO:HALO + bm, t * bn:(t + 1) * bn] = u_ref[t]
    wp = [wp_ref[gi].astype(BF16) for gi in range(len(POOL_WINDOWS))]

    for r0 in range(0, bm, ROW_CHUNK):
        rows = slice(r0, r0 + ROW_CHUNK)
        pos = ((i % blocks_per_seq) * bm + r0
               + lax.broadcasted_iota(jnp.int32, (ROW_CHUNK, 1), 0)).astype(F32)
        groups = []
        for gi, w in enumerate(POOL_WINDOWS):
            c0 = gi * pool_group
            acc = ext_ref[r0:r0 + HALO + ROW_CHUNK, c0:c0 + pool_group]
            cur = acc[HALO:, :]
            span = 1
            while span < w:
                acc = acc + pltpu.roll(acc, span, axis=0)
                span *= 2
            acc = acc[HALO:, :]
            count = jnp.minimum(pos + 1.0, float(w))
            y = (acc / count - cur).astype(BF16)
            y = jnp.dot(y, wp[gi], preferred_element_type=F32)
            groups.append(y * ps_ref[:, c0:c0 + pool_group])
        y_pool = jnp.concatenate(groups, axis=-1)
        y_attn = jnp.concatenate([ya_ref[hh, rows, :] for hh in range(ya_ref.shape[0])], axis=-1)

        n_pool = _rms(y_pool, gp_ref[...]).astype(BF16)
        n_attn = _rms(y_attn, ga_ref[...]).astype(BF16)
        mixed = (jnp.dot(n_pool, wo_ref[0:pool_width, :], preferred_element_type=F32)
                 + jnp.dot(n_attn, wo_ref[pool_width:, :], preferred_element_type=F32))
        h1 = x_ref[rows, :] + _rms(mixed, gm_ref[...])
        h_ref[rows, :] = h1
        hn_ref[rows, :] = _rms(h1, gn_ref[...]).astype(BF16)


def _mix(u_tiles, y_heads, x2, w_pool, pool_scale, g_pool, g_attn, w_out_bf, g_post, g_next, *, seq):
    rows, d = x2.shape
    n_u, _, bn = u_tiles.shape
    n_heads = y_heads.shape[0]
    pool_width = n_u * bn
    attn_width = n_heads * HEAD_DIM
    bm = BM_MIX
    blocks_per_seq = seq // bm
    assert all(w & (w - 1) == 0 and w <= HALO for w in POOL_WINDOWS)
    row_blk = lambda w: pl.BlockSpec((bm, w), lambda i: (i, 0))
    halo_blk = pl.BlockSpec((n_u, HALO, bn), lambda i: (0, jnp.maximum(i * (bm // HALO) - 1, 0), 0))
    kern = functools.partial(_mix_kernel, blocks_per_seq=blocks_per_seq, pool_group=w_pool.shape[-1])
    return pl.pallas_call(
        kern,
        grid=(rows // bm,),
        in_specs=[pl.BlockSpec((n_u, bm, bn), lambda i: (0, i, 0)), halo_blk,
                  pl.BlockSpec((n_heads, bm, HEAD_DIM), lambda i: (0, i, 0)), row_blk(d),
                  _resident(w_pool.shape), _resident((1, pool_width)), _resident((1, pool_width)),
                  _resident((1, attn_width)), _resident(w_out_bf.shape), _resident((1, d)), _resident((1, d))],
        out_specs=[row_blk(d), row_blk(d)],
        out_shape=[jax.ShapeDtypeStruct((rows, d), F32), jax.ShapeDtypeStruct((rows, d), BF16)],
        scratch_shapes=[pltpu.VMEM((HALO + bm, pool_width), F32)],
        compiler_params=pltpu.CompilerParams(dimension_semantics=("arbitrary",),
                                             vmem_limit_bytes=VMEM_LIMIT),
        name="mix_out",
    )(u_tiles, u_tiles, y_heads, x2, w_pool, pool_scale, g_pool, g_attn, w_out_bf, g_post, g_next)


def _mlp_kernel(hn_ref, wu_ref, wd_ref, o_ref):
    def tile(first):
        for c0 in range(0, wu_ref.shape[1], MLP_SUB):
            a = jnp.maximum(jnp.dot(hn_ref[...], wu_ref[:, c0:c0 + MLP_SUB], preferred_element_type=F32), 0.0)
            part = jnp.dot((a * a).astype(BF16), wd_ref[c0:c0 + MLP_SUB, :].astype(BF16),
                           preferred_element_type=F32)
            if first and c0 == 0:
                o_ref[...] = part
            else:
                o_ref[...] += part

    @pl.when(pl.program_id(1) == 0)
    def _():
        tile(True)

    @pl.when(pl.program_id(1) > 0)
    def _():
        tile(False)


def _mlp(hn, w_up_tiles, w_down):
    rows, d = hn.shape
    n_tiles, _, bf = w_up_tiles.shape
    bm = BM_MLP
    return pl.pallas_call(
        _mlp_kernel,
        grid=(rows // bm, n_tiles),
        in_specs=[pl.BlockSpec((bm, d), lambda i, j: (i, 0)),
                  pl.BlockSpec((None, d, bf), lambda i, j: (j, 0, 0)),
                  pl.BlockSpec((bf, d), lambda i, j: (j, 0))],
        out_specs=pl.BlockSpec((bm, d), lambda i, j: (i, 0)),
        out_shape=jax.ShapeDtypeStruct((rows, d), F32),
        compiler_params=pltpu.CompilerParams(dimension_semantics=("arbitrary", "arbitrary"),
                                             vmem_limit_bytes=VMEM_LIMIT),
        name="sqrelu_mlp",
    )(hn, w_up_tiles, w_down)


def _ple_kernel(h_ref, m_ref, p_ref, gm_ref, gg_ref, wg_ref, wp_ref, gpost_ref, o_ref):
    h = h_ref[...] + _rms(m_ref[...], gm_ref[...])
    hn = _rms(h, gg_ref[...]).astype(BF16)
    gate = jax.nn.sigmoid(jnp.dot(hn, wg_ref[...], preferred_element_type=F32))
    e = jnp.dot(p_ref[...].astype(BF16), wp_ref[...].astype(BF16), preferred_element_type=F32)
    o_ref[...] = h + _rms(gate * e, gpost_ref[...])


def _ple(h1, m, p2, g_post_mlp, g_gate, w_gate_bf, w_ple, g_post):
    rows, d = h1.shape
    bm = BM_PLE
    row_blk = lambda w: pl.BlockSpec((bm, w), lambda i: (i, 0))
    return pl.pallas_call(
        _ple_kernel,
        grid=(rows // bm,),
        in_specs=[row_blk(d), row_blk(d), row_blk(p2.shape[1]),
                  _resident((1, d)), _resident((1, d)), _resident(w_gate_bf.shape), _resident(w_ple.shape),
                  _resident((1, d))],
        out_specs=row_blk(d),
        out_shape=jax.ShapeDtypeStruct((rows, d), F32),
        compiler_params=pltpu.CompilerParams(dimension_semantics=("arbitrary",),
                                             vmem_limit_bytes=VMEM_LIMIT),
        name="ple_gate",
    )(h1, m, p2, g_post_mlp, g_gate, w_gate_bf, w_ple, g_post)


def kernel(x, p, g_pre_mix, w_in, b_f, w_pool, pool_scale, g_pool_out, g_attn_out, w_out, g_post_mix,
           g_pre_mlp, w_up, w_down, g_post_mlp, g_ple_gate, w_gate, w_ple, g_post_ple):
    n_batch, seq, d = x.shape
    depth = w_in.shape[0]
    pool_width = pool_scale.shape[-1]
    attn_width = g_attn_out.shape[-1]
    n_heads = b_f.shape[-1]
    rows = n_batch * seq
    vec = lambda a: a.reshape(1, -1)

    h = x.reshape(rows, d)
    for i in range(depth):
        bias_f = jnp.pad(jnp.tile(b_f[i], FGATE_PIECES), (0, LANES - FGATE_PIECES * n_heads)).reshape(1, LANES)
        w_in_t = jnp.swapaxes(w_in[i], 0, 1)
        hn0, f_aug = _prenorm(h, vec(g_pre_mix[i]), w_in_t, bias_f, seq=seq,
                              gate_row0=pool_width + 3 * attn_width, n_heads=n_heads)
        u_tiles, qk_heads, v_t, w_out_bf, w_gate_bf = _inproj(hn0, w_in_t, w_out[i], w_gate[i],
                                                              pool_width=pool_width, attn_width=attn_width)
        y_heads, w_up_tiles = _attention(qk_heads, v_t, f_aug, w_up[i], n_batch=n_batch, n_heads=n_heads)
        h1, hn = _mix(u_tiles, y_heads, h, w_pool[i], vec(pool_scale[i]),
                      vec(g_pool_out[i]), vec(g_attn_out[i]), w_out_bf, vec(g_post_mix[i]),
                      vec(g_pre_mlp[i]), seq=seq)
        m = _mlp(hn, w_up_tiles, w_down[i])
        h = _ple(h1, m, p[i].reshape(rows, -1), vec(g_post_mlp[i]), vec(g_ple_gate[i]), w_gate_bf,
                 w_ple[i], vec(g_post_ple[i]))
    return h.reshape(n_batch, seq, d)
```

```python
import functools

import jax
import jax.numpy as jnp
from jax import lax
from jax.experimental import pallas as pl
from jax.experimental.pallas import tpu as pltpu

F32 = jnp.float32
BF16 = jnp.bfloat16

EPS = 1e-6
HEAD_DIM = 128
POOL_WINDOWS = (2, 4, 8, 16)
LANES = 128
SUBLANES = 8
HALO = 16
LOG2E = 1.4426950408889634
QUERY_SCALE = HEAD_DIM ** -0.5 * LOG2E
FGATE_PIECES = 3

BM_NORM = 1024
BM_PROJ = 2048
BN_PROJ = 512
CUMSUM_BLOCK = 256
BM_MIX = 512
ROW_CHUNK = 256
BM_MLP = 1024
BF_MLP = 2048
MLP_SUB = 512
BM_PLE = 512
BQ = 512
HEADS_PER_STEP = 2

VMEM_LIMIT = 56 * 1024 * 1024


def _rms(x, g):
    var = jnp.mean(x * x, axis=-1, keepdims=True)
    return x * lax.rsqrt(var + EPS) * g


def _log_sigmoid(x):
    return jnp.minimum(x, 0.0) - jnp.log1p(jnp.exp(-jnp.abs(x)))


def _resident(shape):
    return pl.BlockSpec(shape, lambda *_: (0,) * len(shape), pipeline_mode=pl.Buffered(1))


def _dot_t(a, b_t):
    return lax.dot_general(a, b_t, (((1,), (1,)), ((), ())), preferred_element_type=F32)


def _split3(v):
    hi = v.astype(BF16)
    rem = v - hi.astype(F32)
    mid = rem.astype(BF16)
    lo = (rem - mid.astype(F32)).astype(BF16)
    return hi, mid, lo


def _prenorm_kernel(x_ref, g_ref, wf_ref, bf_ref, hn_ref, fa_ref, carry_ref, *, blocks_per_seq, n_heads):
    i = pl.program_id(0)
    bm, d = x_ref.shape
    hn = _rms(x_ref[...], g_ref[...]).astype(BF16)
    hn_ref[...] = hn
    wf = wf_ref[...]
    wf_rep = jnp.concatenate([wf] * FGATE_PIECES + [jnp.zeros((LANES - FGATE_PIECES * n_heads, d), F32)],
                             axis=0).astype(BF16)
    logf = _log_sigmoid(_dot_t(hn, wf_rep) + bf_ref[...])
    n = CUMSUM_BLOCK
    row = lax.broadcasted_iota(jnp.int32, (n, n), 0)
    col = lax.broadcasted_iota(jnp.int32, (n, n), 1)
    tri = (col <= row).astype(BF16)
    lane = lax.broadcasted_iota(jnp.int32, (n, LANES), 1)
    carry = jnp.where(i % blocks_per_seq == 0, 0.0, carry_ref[...])
    for sb in range(bm // n):
        hi, mid, lo = _split3(logf[sb * n:(sb + 1) * n, :])
        cs = (jnp.dot(tri, hi, preferred_element_type=F32)
              + jnp.dot(tri, mid, preferred_element_type=F32)
              + jnp.dot(tri, lo, preferred_element_type=F32)) + carry
        carry = cs[n - 1:n, :]
        hi, mid, lo = _split3(cs * (-LOG2E))
        zero = jnp.zeros_like(hi)
        fa_ref[sb * n:(sb + 1) * n, :] = jnp.where(
            lane < n_heads, hi, jnp.where(lane < 2 * n_heads, mid, jnp.where(lane < 3 * n_heads, lo, zero)))
    carry_ref[...] = carry


def _prenorm(x2, g, w_in_t, b_f, *, seq, gate_row0, n_heads):
    rows, d = x2.shape
    bm = BM_NORM
    assert n_heads == SUBLANES and FGATE_PIECES * n_heads <= LANES
    row_blk = lambda w: pl.BlockSpec((bm, w), lambda i: (i, 0))
    kern = functools.partial(_prenorm_kernel, blocks_per_seq=seq // bm, n_heads=n_heads)
    return pl.pallas_call(
        kern,
        grid=(rows // bm,),
        in_specs=[row_blk(d), _resident((1, d)),
                  pl.BlockSpec((n_heads, d), lambda i: (gate_row0 // n_heads, 0)),
                  _resident((1, LANES))],
        out_specs=[row_blk(d), row_blk(LANES)],
        out_shape=[jax.ShapeDtypeStruct((rows, d), BF16), jax.ShapeDtypeStruct((rows, LANES), BF16)],
        scratch_shapes=[pltpu.VMEM((1, LANES), F32)],
        compiler_params=pltpu.CompilerParams(dimension_semantics=("arbitrary",),
                                             vmem_limit_bytes=VMEM_LIMIT),
        name="prenorm_gates",
    )(x2, g, w_in_t, b_f)


def _inproj_kernel(hn_ref, w_ref, wo_ref, wg_ref, u_ref, qk_ref, vt_ref, wo_bf_ref, wg_bf_ref,
                   *, u_tiles, q_tiles):
    j = pl.program_id(1)
    v_start = u_tiles + 2 * q_tiles

    wo_bf_ref[...] = wo_ref[...].astype(BF16)
    wg_bf_ref[...] = wg_ref[...].astype(BF16)

    @pl.when(j < u_tiles)
    def _():
        u_ref[...] = _dot_t(hn_ref[...], w_ref[...].astype(BF16))

    @pl.when((j >= u_tiles) & (j < v_start))
    def _():
        c = jnp.where(j < u_tiles + q_tiles, QUERY_SCALE, 1.0)
        z = (_dot_t(hn_ref[...], w_ref[...].astype(BF16)) * c).astype(BF16)
        for hh in range(qk_ref.shape[0]):
            qk_ref[hh] = z[:, hh * HEAD_DIM:(hh + 1) * HEAD_DIM]

    @pl.when(j >= v_start)
    def _():
        vt_ref[...] = _dot_t(w_ref[...].astype(BF16), hn_ref[...]).astype(BF16)


def _inproj(hn, w_in_t, w_out, w_gate, *, pool_width, attn_width):
    rows, d = hn.shape
    bm, bn = BM_PROJ, BN_PROJ
    u_tiles = pool_width // bn
    q_tiles = attn_width // bn
    heads_per_tile = bn // HEAD_DIM
    n_tiles = (pool_width + 3 * attn_width) // bn
    n_steps = (rows // bm) * n_tiles
    cast_rows = w_out.shape[0] // n_steps
    cast_blk = pl.BlockSpec((cast_rows, w_out.shape[1]), lambda i, j: (i * n_tiles + j, 0))
    kern = functools.partial(_inproj_kernel, u_tiles=u_tiles, q_tiles=q_tiles)
    return pl.pallas_call(
        kern,
        grid=(rows // bm, n_tiles),
        in_specs=[pl.BlockSpec((bm, d), lambda i, j: (i, 0)),
                  pl.BlockSpec((bn, d), lambda i, j: (j, 0)),
                  cast_blk, cast_blk],
        out_specs=[pl.BlockSpec((None, bm, bn), lambda i, j: (jnp.minimum(j, u_tiles - 1), i, 0)),
                   pl.BlockSpec((heads_per_tile, bm, HEAD_DIM),
                                lambda i, j: (jnp.clip(j - u_tiles, 0, 2 * q_tiles - 1), i, 0)),
                   pl.BlockSpec((bn, bm), lambda i, j: (jnp.maximum(j - u_tiles - 2 * q_tiles, 0), i)),
                   cast_blk, cast_blk],
        out_shape=[jax.ShapeDtypeStruct((u_tiles, rows, bn), F32),
                   jax.ShapeDtypeStruct((2 * attn_width // HEAD_DIM, rows, HEAD_DIM), BF16),
                   jax.ShapeDtypeStruct((attn_width, rows), BF16),
                   jax.ShapeDtypeStruct(w_out.shape, BF16),
                   jax.ShapeDtypeStruct(w_gate.shape, BF16)],
        compiler_params=pltpu.CompilerParams(dimension_semantics=("arbitrary", "arbitrary"),
                                             vmem_limit_bytes=VMEM_LIMIT),
        name="inproj",
    )(hn, w_in_t, w_out, w_gate)


def _attn_kernel(q_ref, k_ref, vt_ref, fa_ref, wu_ref, o_ref, wu_bf_ref, kaug_ref, *s_refs, n_heads):
    heads_here, seq, _ = q_ref.shape
    n_blk = seq // BQ
    wu_bf_ref[...] = wu_ref[...].astype(BF16)
    lane = lax.broadcasted_iota(jnp.int32, (BQ, LANES), 1)
    krow = lax.broadcasted_iota(jnp.int32, (BQ, BQ), 0)
    qcol = lax.broadcasted_iota(jnp.int32, (BQ, BQ), 1)
    causal = krow <= qcol
    maxes = []
    for hh in range(heads_here):
        h = pl.program_id(1) * heads_here + hh
        kaug_ref[hh, :, 0:HEAD_DIM] = k_ref[hh]
        kaug_ref[hh, :, HEAD_DIM:] = fa_ref[...]
        pick = (lane == h) | (lane == h + n_heads) | (lane == h + 2 * n_heads)
        onehot = jnp.where(pick, 1.0, 0.0).astype(BF16)
        for blk in range(n_blk):
            start, end = blk * BQ, (blk + 1) * BQ
            s_ref = s_refs[hh * n_blk + blk]
            q_aug = jnp.concatenate([q_ref[hh, start:end, :], onehot], axis=1)
            s = _dot_t(kaug_ref[hh, 0:end, :], q_aug)
            s_diag = jnp.where(causal, s[start:end, :], -jnp.inf)
            s_ref[start:end, :] = s_diag
            m = jnp.max(s_diag, axis=0, keepdims=True)
            if start:
                s_ref[0:start, :] = s[0:start, :]
                m = jnp.maximum(m, jnp.max(s[0:start, :], axis=0, keepdims=True))
            maxes.append(m)
    for hh in range(heads_here):
        for blk in range(n_blk):
            s_ref, m = s_refs[hh * n_blk + blk], maxes[hh * n_blk + blk]
            denom = jnp.zeros((1, BQ), F32)
            o_t = jnp.zeros((HEAD_DIM, BQ), F32)
            for c in range(blk + 1):
                keys = slice(c * BQ, (c + 1) * BQ)
                p = jnp.exp2(s_ref[keys, :] - m)
                denom = denom + jnp.sum(p, axis=0, keepdims=True)
                o_t = o_t + jnp.dot(vt_ref[hh * HEAD_DIM:(hh + 1) * HEAD_DIM, keys], p.astype(BF16),
                                    preferred_element_type=F32)
            o_ref[hh, blk * BQ:(blk + 1) * BQ, :] = (o_t / denom).T


def _attention(qk_heads, v_t, f_aug, w_up, *, n_batch, n_heads):
    _, rows, _ = qk_heads.shape
    seq = rows // n_batch
    d, d_ff = w_up.shape
    n_tiles = d_ff // BF_MLP
    hps = HEADS_PER_STEP
    groups = n_heads // hps
    parts = (n_batch * groups) // n_tiles
    part_rows = d // parts
    head_blk = lambda part: pl.BlockSpec((hps, seq, HEAD_DIM), lambda b, g: (part * groups + g, b, 0))
    step = lambda b, g: b * groups + g
    return pl.pallas_call(
        functools.partial(_attn_kernel, n_heads=n_heads),
        grid=(n_batch, groups),
        in_specs=[head_blk(0), head_blk(1),
                  pl.BlockSpec((hps * HEAD_DIM, seq), lambda b, g: (g, b)),
                  pl.BlockSpec((seq, LANES), lambda b, g: (b, 0)),
                  pl.BlockSpec((part_rows, BF_MLP), lambda b, g: (step(b, g) % parts, step(b, g) // parts))],
        out_specs=[head_blk(0),
                   pl.BlockSpec((None, part_rows, BF_MLP), lambda b, g: (step(b, g) // parts, step(b, g) % parts, 0))],
        out_shape=[jax.ShapeDtypeStruct((n_heads, rows, HEAD_DIM), F32),
                   jax.ShapeDtypeStruct((n_tiles, d, BF_MLP), BF16)],
        scratch_shapes=[pltpu.VMEM((hps, seq, HEAD_DIM + LANES), BF16)]
                       + [pltpu.VMEM(((blk + 1) * BQ, BQ), F32) for _ in range(hps) for blk in range(seq // BQ)],
        compiler_params=pltpu.CompilerParams(dimension_semantics=("arbitrary", "arbitrary"),
                                             vmem_limit_bytes=VMEM_LIMIT),
        name="fox_attention",
    )(qk_heads, qk_heads, v_t, f_aug, w_up)


def _mix_kernel(u_ref, halo_ref, ya_ref, x_ref, wp_ref, ps_ref, gp_ref, ga_ref, wo_ref, gm_ref, gn_ref,
                h_ref, hn_ref, ext_ref, *, blocks_per_seq, pool_group):
    i = pl.program_id(0)
    u_tiles, bm, bn = u_ref.shape
    pool_width = u_tiles * bn
    first = (i % blocks_per_seq) == 0
    for t in range(u_tiles):
        ext_ref[0:HALO, t * bn:(t + 1) * bn] = jnp.where(first, 0.0, halo_ref[t])
        ext_ref[HALO:HALO + bm, t * bn:(t + 1) * bn] = u_ref[t]
    wp = [wp_ref[gi].astype(BF16) for gi in range(len(POOL_WINDOWS))]

    for r0 in range(0, bm, ROW_CHUNK):
        rows = slice(r0, r0 + ROW_CHUNK)
        pos = ((i % blocks_per_seq) * bm + r0
               + lax.broadcasted_iota(jnp.int32, (ROW_CHUNK, 1), 0)).astype(F32)
        groups = []
        for gi, w in enumerate(POOL_WINDOWS):
            c0 = gi * pool_group
            acc = ext_ref[r0:r0 + HALO + ROW_CHUNK, c0:c0 + pool_group]
            cur = acc[HALO:, :]
            span = 1
            while span < w:
                acc = acc + pltpu.roll(acc, span, axis=0)
                span *= 2
            acc = acc[HALO:, :]
            count = jnp.minimum(pos + 1.0, float(w))
            y = (acc / count - cur).astype(BF16)
            y = jnp.dot(y, wp[gi], preferred_element_type=F32)
            groups.append(y * ps_ref[:, c0:c0 + pool_group])
        y_pool = jnp.concatenate(groups, axis=-1)
        y_attn = jnp.concatenate([ya_ref[hh, rows, :] for hh in range(ya_ref.shape[0])], axis=-1)

        n_pool = _rms(y_pool, gp_ref[...]).astype(BF16)
        n_attn = _rms(y_attn, ga_ref[...]).astype(BF16)
        mixed = (jnp.dot(n_pool, wo_ref[0:pool_width, :], preferred_element_type=F32)
                 + jnp.dot(n_attn, wo_ref[pool_width:, :], preferred_element_type=F32))
        h1 = x_ref[rows, :] + _rms(mixed, gm_ref[...])
        h_ref[rows, :] = h1
        hn_ref[rows, :] = _rms(h1, gn_ref[...]).astype(BF16)


def _mix(u_tiles, y_heads, x2, w_pool, pool_scale, g_pool, g_attn, w_out_bf, g_post, g_next, *, seq):
    rows, d = x2.shape
    n_u, _, bn = u_tiles.shape
    n_heads = y_heads.shape[0]
    pool_width = n_u * bn
    attn_width = n_heads * HEAD_DIM
    bm = BM_MIX
    blocks_per_seq = seq // bm
    assert all(w & (w - 1) == 0 and w <= HALO for w in POOL_WINDOWS)
    row_blk = lambda w: pl.BlockSpec((bm, w), lambda i: (i, 0))
    halo_blk = pl.BlockSpec((n_u, HALO, bn), lambda i: (0, jnp.maximum(i * (bm // HALO) - 1, 0), 0))
    kern = functools.partial(_mix_kernel, blocks_per_seq=blocks_per_seq, pool_group=w_pool.shape[-1])
    return pl.pallas_call(
        kern,
        grid=(rows // bm,),
        in_specs=[pl.BlockSpec((n_u, bm, bn), lambda i: (0, i, 0)), halo_blk,
                  pl.BlockSpec((n_heads, bm, HEAD_DIM), lambda i: (0, i, 0)), row_blk(d),
                  _resident(w_pool.shape), _resident((1, pool_width)), _resident((1, pool_width)),
                  _resident((1, attn_width)), _resident(w_out_bf.shape), _resident((1, d)), _resident((1, d))],
        out_specs=[row_blk(d), row_blk(d)],
        out_shape=[jax.ShapeDtypeStruct((rows, d), F32), jax.ShapeDtypeStruct((rows, d), BF16)],
        scratch_shapes=[pltpu.VMEM((HALO + bm, pool_width), F32)],
        compiler_params=pltpu.CompilerParams(dimension_semantics=("arbitrary",),
                                             vmem_limit_bytes=VMEM_LIMIT),
        name="mix_out",
    )(u_tiles, u_tiles, y_heads, x2, w_pool, pool_scale, g_pool, g_attn, w_out_bf, g_post, g_next)


def _mlp_kernel(hn_ref, wu_ref, wd_ref, o_ref):
    def tile(first):
        for c0 in range(0, wu_ref.shape[1], MLP_SUB):
            a = jnp.maximum(jnp.dot(hn_ref[...], wu_ref[:, c0:c0 + MLP_SUB], preferred_element_type=F32), 0.0)
            part = jnp.dot((a * a).astype(BF16), wd_ref[c0:c0 + MLP_SUB, :].astype(BF16),
                           preferred_element_type=F32)
            if first and c0 == 0:
                o_ref[...] = part
            else:
                o_ref[...] += part

    @pl.when(pl.program_id(1) == 0)
    def _():
        tile(True)

    @pl.when(pl.program_id(1) > 0)
    def _():
        tile(False)


def _mlp(hn, w_up_tiles, w_down):
    rows, d = hn.shape
    n_tiles, _, bf = w_up_tiles.shape
    bm = BM_MLP
    return pl.pallas_call(
        _mlp_kernel,
        grid=(rows // bm, n_tiles),
        in_specs=[pl.BlockSpec((bm, d), lambda i, j: (i, 0)),
                  pl.BlockSpec((None, d, bf), lambda i, j: (j, 0, 0)),
                  pl.BlockSpec((bf, d), lambda i, j: (j, 0))],
        out_specs=pl.BlockSpec((bm, d), lambda i, j: (i, 0)),
        out_shape=jax.ShapeDtypeStruct((rows, d), F32),
        compiler_params=pltpu.CompilerParams(dimension_semantics=("arbitrary", "arbitrary"),
                                             vmem_limit_bytes=60 * 1024 * 1024),
        name="sqrelu_mlp",
    )(hn, w_up_tiles, w_down)


def _ple_kernel(h_ref, m_ref, p_ref, gm_ref, gg_ref, wg_ref, wp_ref, gpost_ref, o_ref):
    h = h_ref[...] + _rms(m_ref[...], gm_ref[...])
    hn = _rms(h, gg_ref[...]).astype(BF16)
    gate = jax.nn.sigmoid(jnp.dot(hn, wg_ref[...], preferred_element_type=F32))
    e = jnp.dot(p_ref[...].astype(BF16), wp_ref[...].astype(BF16), preferred_element_type=F32)
    o_ref[...] = h + _rms(gate * e, gpost_ref[...])


def _ple(h1, m, p2, g_post_mlp, g_gate, w_gate_bf, w_ple, g_post):
    rows, d = h1.shape
    bm = BM_PLE
    row_blk = lambda w: pl.BlockSpec((bm, w), lambda i: (i, 0))
    return pl.pallas_call(
        _ple_kernel,
        grid=(rows // bm,),
        in_specs=[row_blk(d), row_blk(d), row_blk(p2.shape[1]),
                  _resident((1, d)), _resident((1, d)), _resident(w_gate_bf.shape), _resident(w_ple.shape),
                  _resident((1, d))],
        out_specs=row_blk(d),
        out_shape=jax.ShapeDtypeStruct((rows, d), F32),
        compiler_params=pltpu.CompilerParams(dimension_semantics=("arbitrary",),
                                             vmem_limit_bytes=VMEM_LIMIT),
        name="ple_gate",
    )(h1, m, p2, g_post_mlp, g_gate, w_gate_bf, w_ple, g_post)


def kernel(x, p, g_pre_mix, w_in, b_f, w_pool, pool_scale, g_pool_out, g_attn_out, w_out, g_post_mix,
           g_pre_mlp, w_up, w_down, g_post_mlp, g_ple_gate, w_gate, w_ple, g_post_ple):
    n_batch, seq, d = x.shape
    depth = w_in.shape[0]
    pool_width = pool_scale.shape[-1]
    attn_width = g_attn_out.shape[-1]
    n_heads = b_f.shape[-1]
    rows = n_batch * seq
    vec = lambda a: a.reshape(1, -1)

    h = x.reshape(rows, d)
    for i in range(depth):
        bias_f = jnp.pad(jnp.tile(b_f[i], FGATE_PIECES), (0, LANES - FGATE_PIECES * n_heads)).reshape(1, LANES)
        w_in_t = jnp.swapaxes(w_in[i], 0, 1)
        hn0, f_aug = _prenorm(h, vec(g_pre_mix[i]), w_in_t, bias_f, seq=seq,
                              gate_row0=pool_width + 3 * attn_width, n_heads=n_heads)
        u_tiles, qk_heads, v_t, w_out_bf, w_gate_bf = _inproj(hn0, w_in_t, w_out[i], w_gate[i],
                                                              pool_width=pool_width, attn_width=attn_width)
        y_heads, w_up_tiles = _attention(qk_heads, v_t, f_aug, w_up[i], n_batch=n_batch, n_heads=n_heads)
        h1, hn = _mix(u_tiles, y_heads, h, w_pool[i], vec(pool_scale[i]),
                      vec(g_pool_out[i]), vec(g_attn_out[i]), w_out_bf, vec(g_post_mix[i]),
                      vec(g_pre_mlp[i]), seq=seq)
        m = _mlp(hn, w_up_tiles, w_down[i].astype(BF16))
        h = _ple(h1, m, p[i].reshape(rows, -1), vec(g_post_mlp[i]), vec(g_ple_gate[i]), w_gate_bf,
                 w_ple[i], vec(g_post_ple[i]))
    return h.reshape(n_batch, seq, d)
```

```python
import functools

import jax
import jax.numpy as jnp
from jax import lax
from jax.experimental import pallas as pl
from jax.experimental.pallas import tpu as pltpu

F32 = jnp.float32
BF16 = jnp.bfloat16

EPS = 1e-6
HEAD_DIM = 128
POOL_WINDOWS = (2, 4, 8, 16)
LANES = 128
SUBLANES = 8
HALO = 16
LOG2E = 1.4426950408889634
QUERY_SCALE = HEAD_DIM ** -0.5 * LOG2E
FGATE_PIECES = 3

BM_NORM = 1024
BM_PROJ = 2048
BN_PROJ = 512
CUMSUM_BLOCK = 256
BM_MIX = 512
ROW_CHUNK = 256
BM_MLP = 1024
BF_MLP = 1024
MLP_SUB = 1024
BM_PLE = 512
BQ = 512
HEADS_PER_STEP = 2

VMEM_LIMIT = 56 * 1024 * 1024


def _rms(x, g):
    var = jnp.mean(x * x, axis=-1, keepdims=True)
    return x * lax.rsqrt(var + EPS) * g


def _log_sigmoid(x):
    return jnp.minimum(x, 0.0) - jnp.log1p(jnp.exp(-jnp.abs(x)))


def _resident(shape):
    return pl.BlockSpec(shape, lambda *_: (0,) * len(shape), pipeline_mode=pl.Buffered(1))


def _dot_t(a, b_t):
    return lax.dot_general(a, b_t, (((1,), (1,)), ((), ())), preferred_element_type=F32)


def _split3(v):
    hi = v.astype(BF16)
    rem = v - hi.astype(F32)
    mid = rem.astype(BF16)
    lo = (rem - mid.astype(F32)).astype(BF16)
    return hi, mid, lo


def _prenorm_kernel(x_ref, g_ref, wf_ref, bf_ref, hn_ref, fa_ref, carry_ref, *, blocks_per_seq, n_heads):
    i = pl.program_id(0)
    bm, d = x_ref.shape
    hn = _rms(x_ref[...], g_ref[...]).astype(BF16)
    hn_ref[...] = hn
    wf = wf_ref[...]
    wf_rep = jnp.concatenate([wf] * FGATE_PIECES + [jnp.zeros((LANES - FGATE_PIECES * n_heads, d), F32)],
                             axis=0).astype(BF16)
    logf = _log_sigmoid(_dot_t(hn, wf_rep) + bf_ref[...])
    n = CUMSUM_BLOCK
    row = lax.broadcasted_iota(jnp.int32, (n, n), 0)
    col = lax.broadcasted_iota(jnp.int32, (n, n), 1)
    tri = (col <= row).astype(BF16)
    lane = lax.broadcasted_iota(jnp.int32, (n, LANES), 1)
    carry = jnp.where(i % blocks_per_seq == 0, 0.0, carry_ref[...])
    for sb in range(bm // n):
        hi, mid, lo = _split3(logf[sb * n:(sb + 1) * n, :])
        cs = (jnp.dot(tri, hi, preferred_element_type=F32)
              + jnp.dot(tri, mid, preferred_element_type=F32)
              + jnp.dot(tri, lo, preferred_element_type=F32)) + carry
        carry = cs[n - 1:n, :]
        hi, mid, lo = _split3(cs * (-LOG2E))
        zero = jnp.zeros_like(hi)
        fa_ref[sb * n:(sb + 1) * n, :] = jnp.where(
            lane < n_heads, hi, jnp.where(lane < 2 * n_heads, mid, jnp.where(lane < 3 * n_heads, lo, zero)))
    carry_ref[...] = carry


def _prenorm(x2, g, w_in_t, b_f, *, seq, gate_row0, n_heads):
    rows, d = x2.shape
    bm = BM_NORM
    assert n_heads == SUBLANES and FGATE_PIECES * n_heads <= LANES
    row_blk = lambda w: pl.BlockSpec((bm, w), lambda i: (i, 0))
    kern = functools.partial(_prenorm_kernel, blocks_per_seq=seq // bm, n_heads=n_heads)
    return pl.pallas_call(
        kern,
        grid=(rows // bm,),
        in_specs=[row_blk(d), _resident((1, d)),
                  pl.BlockSpec((n_heads, d), lambda i: (gate_row0 // n_heads, 0)),
                  _resident((1, LANES))],
        out_specs=[row_blk(d), row_blk(LANES)],
        out_shape=[jax.ShapeDtypeStruct((rows, d), BF16), jax.ShapeDtypeStruct((rows, LANES), BF16)],
        scratch_shapes=[pltpu.VMEM((1, LANES), F32)],
        compiler_params=pltpu.CompilerParams(dimension_semantics=("arbitrary",),
                                             vmem_limit_bytes=VMEM_LIMIT),
        name="prenorm_gates",
    )(x2, g, w_in_t, b_f)


def _inproj_kernel(hn_ref, w_ref, wo_ref, wg_ref, u_ref, qk_ref, vt_ref, wo_bf_ref, wg_bf_ref,
                   *, u_tiles, q_tiles):
    j = pl.program_id(1)
    v_start = u_tiles + 2 * q_tiles

    wo_bf_ref[...] = wo_ref[...].astype(BF16)
    wg_bf_ref[...] = wg_ref[...].astype(BF16)

    @pl.when(j < u_tiles)
    def _():
        u_ref[...] = _dot_t(hn_ref[...], w_ref[...].astype(BF16))

    @pl.when((j >= u_tiles) & (j < v_start))
    def _():
        c = jnp.where(j < u_tiles + q_tiles, QUERY_SCALE, 1.0)
        z = (_dot_t(hn_ref[...], w_ref[...].astype(BF16)) * c).astype(BF16)
        for hh in range(qk_ref.shape[0]):
            qk_ref[hh] = z[:, hh * HEAD_DIM:(hh + 1) * HEAD_DIM]

    @pl.when(j >= v_start)
    def _():
        vt_ref[...] = _dot_t(w_ref[...].astype(BF16), hn_ref[...]).astype(BF16)


def _inproj(hn, w_in_t, w_out, w_gate, *, pool_width, attn_width):
    rows, d = hn.shape
    bm, bn = BM_PROJ, BN_PROJ
    u_tiles = pool_width // bn
    q_tiles = attn_width // bn
    heads_per_tile = bn // HEAD_DIM
    n_tiles = (pool_width + 3 * attn_width) // bn
    n_steps = (rows // bm) * n_tiles
    cast_rows = w_out.shape[0] // n_steps
    cast_blk = pl.BlockSpec((cast_rows, w_out.shape[1]), lambda i, j: (i * n_tiles + j, 0))
    kern = functools.partial(_inproj_kernel, u_tiles=u_tiles, q_tiles=q_tiles)
    return pl.pallas_call(
        kern,
        grid=(rows // bm, n_tiles),
        in_specs=[pl.BlockSpec((bm, d), lambda i, j: (i, 0)),
                  pl.BlockSpec((bn, d), lambda i, j: (j, 0)),
                  cast_blk, cast_blk],
        out_specs=[pl.BlockSpec((None, bm, bn), lambda i, j: (jnp.minimum(j, u_tiles - 1), i, 0)),
                   pl.BlockSpec((heads_per_tile, bm, HEAD_DIM),
                                lambda i, j: (jnp.clip(j - u_tiles, 0, 2 * q_tiles - 1), i, 0)),
                   pl.BlockSpec((bn, bm), lambda i, j: (jnp.maximum(j - u_tiles - 2 * q_tiles, 0), i)),
                   cast_blk, cast_blk],
        out_shape=[jax.ShapeDtypeStruct((u_tiles, rows, bn), F32),
                   jax.ShapeDtypeStruct((2 * attn_width // HEAD_DIM, rows, HEAD_DIM), BF16),
                   jax.ShapeDtypeStruct((attn_width, rows), BF16),
                   jax.ShapeDtypeStruct(w_out.shape, BF16),
                   jax.ShapeDtypeStruct(w_gate.shape, BF16)],
        compiler_params=pltpu.CompilerParams(dimension_semantics=("arbitrary", "arbitrary"),
                                             vmem_limit_bytes=VMEM_LIMIT),
        name="inproj",
    )(hn, w_in_t, w_out, w_gate)


def _attn_kernel(q_ref, k_ref, vt_ref, fa_ref, wu_ref, o_ref, wu_bf_ref, kaug_ref, *s_refs, n_heads):
    heads_here, seq, _ = q_ref.shape
    n_blk = seq // BQ
    wu_bf_ref[...] = wu_ref[...].astype(BF16)
    lane = lax.broadcasted_iota(jnp.int32, (BQ, LANES), 1)
    krow = lax.broadcasted_iota(jnp.int32, (BQ, BQ), 0)
    qcol = lax.broadcasted_iota(jnp.int32, (BQ, BQ), 1)
    causal = krow <= qcol
    maxes = []
    for hh in range(heads_here):
        h = pl.program_id(1) * heads_here + hh
        kaug_ref[hh, :, 0:HEAD_DIM] = k_ref[hh]
        kaug_ref[hh, :, HEAD_DIM:] = fa_ref[...]
        pick = (lane == h) | (lane == h + n_heads) | (lane == h + 2 * n_heads)
        onehot = jnp.where(pick, 1.0, 0.0).astype(BF16)
        for blk in range(n_blk):
            start, end = blk * BQ, (blk + 1) * BQ
            s_ref = s_refs[hh * n_blk + blk]
            q_aug = jnp.concatenate([q_ref[hh, start:end, :], onehot], axis=1)
            s = _dot_t(kaug_ref[hh, 0:end, :], q_aug)
            s_diag = jnp.where(causal, s[start:end, :], -jnp.inf)
            s_ref[start:end, :] = s_diag
            m = jnp.max(s_diag, axis=0, keepdims=True)
            if start:
                s_ref[0:start, :] = s[0:start, :]
                m = jnp.maximum(m, jnp.max(s[0:start, :], axis=0, keepdims=True))
            maxes.append(m)
    for hh in range(heads_here):
        for blk in range(n_blk):
            s_ref, m = s_refs[hh * n_blk + blk], maxes[hh * n_blk + blk]
            denom = jnp.zeros((1, BQ), F32)
            o_t = jnp.zeros((HEAD_DIM, BQ), F32)
            for c in range(blk + 1):
                keys = slice(c * BQ, (c + 1) * BQ)
                p = jnp.exp2(s_ref[keys, :] - m)
                denom = denom + jnp.sum(p, axis=0, keepdims=True)
                o_t = o_t + jnp.dot(vt_ref[hh * HEAD_DIM:(hh + 1) * HEAD_DIM, keys], p.astype(BF16),
                                    preferred_element_type=F32)
            o_ref[hh, blk * BQ:(blk + 1) * BQ, :] = (o_t / denom).T


def _attention(qk_heads, v_t, f_aug, w_up, *, n_batch, n_heads):
    _, rows, _ = qk_heads.shape
    seq = rows // n_batch
    d, d_ff = w_up.shape
    n_tiles = d_ff // BF_MLP
    hps = HEADS_PER_STEP
    groups = n_heads // hps
    parts = (n_batch * groups) // n_tiles
    part_rows = d // parts
    head_blk = lambda part: pl.BlockSpec((hps, seq, HEAD_DIM), lambda b, g: (part * groups + g, b, 0))
    step = lambda b, g: b * groups + g
    return pl.pallas_call(
        functools.partial(_attn_kernel, n_heads=n_heads),
        grid=(n_batch, groups),
        in_specs=[head_blk(0), head_blk(1),
                  pl.BlockSpec((hps * HEAD_DIM, seq), lambda b, g: (g, b)),
                  pl.BlockSpec((seq, LANES), lambda b, g: (b, 0)),
                  pl.BlockSpec((part_rows, BF_MLP), lambda b, g: (step(b, g) % parts, step(b, g) // parts))],
        out_specs=[head_blk(0),
                   pl.BlockSpec((None, part_rows, BF_MLP), lambda b, g: (step(b, g) // parts, step(b, g) % parts, 0))],
        out_shape=[jax.ShapeDtypeStruct((n_heads, rows, HEAD_DIM), F32),
                   jax.ShapeDtypeStruct((n_tiles, d, BF_MLP), BF16)],
        scratch_shapes=[pltpu.VMEM((hps, seq, HEAD_DIM + LANES), BF16)]
                       + [pltpu.VMEM(((blk + 1) * BQ, BQ), F32) for _ in range(hps) for blk in range(seq // BQ)],
        compiler_params=pltpu.CompilerParams(dimension_semantics=("arbitrary", "arbitrary"),
                                             vmem_limit_bytes=VMEM_LIMIT),
        name="fox_attention",
    )(qk_heads, qk_heads, v_t, f_aug, w_up)


def _mix_kernel(u_ref, halo_ref, ya_ref, x_ref, wp_ref, ps_ref, gp_ref, ga_ref, wo_ref, gm_ref, gn_ref,
                h_ref, hn_ref, *, blocks_per_seq, pool_group):
    i = pl.program_id(0)
    u_tiles, bm, bn = u_ref.shape
    pool_width = u_tiles * bn
    first = (i % blocks_per_seq) == 0
    wp = [wp_ref[gi].astype(BF16) for gi in range(len(POOL_WINDOWS))]

    for r0 in range(0, bm, ROW_CHUNK):
        rows = slice(r0, r0 + ROW_CHUNK)
        pos = ((i % blocks_per_seq) * bm + r0
               + lax.broadcasted_iota(jnp.int32, (ROW_CHUNK, 1), 0)).astype(F32)
        groups = []
        for gi, w in enumerate(POOL_WINDOWS):
            c0 = gi * pool_group
            t, cols = c0 // bn, slice(c0 % bn, c0 % bn + pool_group)
            cur = u_ref[t, rows, cols]
            if r0:
                back = u_ref[t, r0 - HALO:r0, cols]
            else:
                back = jnp.where(first, 0.0, halo_ref[t, :, cols])
            acc = jnp.concatenate([back, cur], axis=0)
            span = 1
            while span < w:
                acc = acc + pltpu.roll(acc, span, axis=0)
                span *= 2
            acc = acc[HALO:, :]
            count = jnp.minimum(pos + 1.0, float(w))
            y = (acc / count - cur).astype(BF16)
            y = jnp.dot(y, wp[gi], preferred_element_type=F32)
            groups.append(y * ps_ref[:, c0:c0 + pool_group])
        y_pool = jnp.concatenate(groups, axis=-1)
        y_attn = jnp.concatenate([ya_ref[hh, rows, :] for hh in range(ya_ref.shape[0])], axis=-1)

        n_pool = _rms(y_pool, gp_ref[...]).astype(BF16)
        n_attn = _rms(y_attn, ga_ref[...]).astype(BF16)
        mixed = (jnp.dot(n_pool, wo_ref[0:pool_width, :], preferred_element_type=F32)
                 + jnp.dot(n_attn, wo_ref[pool_width:, :], preferred_element_type=F32))
        h1 = x_ref[rows, :] + _rms(mixed, gm_ref[...])
        h_ref[rows, :] = h1
        hn_ref[rows, :] = _rms(h1, gn_ref[...]).astype(BF16)


def _mix(u_tiles, y_heads, x2, w_pool, pool_scale, g_pool, g_attn, w_out_bf, g_post, g_next, *, seq):
    rows, d = x2.shape
    n_u, _, bn = u_tiles.shape
    n_heads = y_heads.shape[0]
    pool_width = n_u * bn
    attn_width = n_heads * HEAD_DIM
    bm = BM_MIX
    blocks_per_seq = seq // bm
    assert all(w & (w - 1) == 0 and w <= HALO for w in POOL_WINDOWS)
    row_blk = lambda w: pl.BlockSpec((bm, w), lambda i: (i, 0))
    halo_blk = pl.BlockSpec((n_u, HALO, bn), lambda i: (0, jnp.maximum(i * (bm // HALO) - 1, 0), 0))
    kern = functools.partial(_mix_kernel, blocks_per_seq=blocks_per_seq, pool_group=w_pool.shape[-1])
    return pl.pallas_call(
        kern,
        grid=(rows // bm,),
        in_specs=[pl.BlockSpec((n_u, bm, bn), lambda i: (0, i, 0)), halo_blk,
                  pl.BlockSpec((n_heads, bm, HEAD_DIM), lambda i: (0, i, 0)), row_blk(d),
                  _resident(w_pool.shape), _resident((1, pool_width)), _resident((1, pool_width)),
                  _resident((1, attn_width)), _resident(w_out_bf.shape), _resident((1, d)), _resident((1, d))],
        out_specs=[row_blk(d), row_blk(d)],
        out_shape=[jax.ShapeDtypeStruct((rows, d), F32), jax.ShapeDtypeStruct((rows, d), BF16)],
        compiler_params=pltpu.CompilerParams(dimension_semantics=("arbitrary",),
                                             vmem_limit_bytes=VMEM_LIMIT),
        name="mix_out",
    )(u_tiles, u_tiles, y_heads, x2, w_pool, pool_scale, g_pool, g_attn, w_out_bf, g_post, g_next)


def _mlp_kernel(hn_ref, wu_ref, wd_ref, o_ref):
    def tile(first):
        for c0 in range(0, wu_ref.shape[1], MLP_SUB):
            a = jnp.maximum(jnp.dot(hn_ref[...], wu_ref[:, c0:c0 + MLP_SUB], preferred_element_type=F32), 0.0)
            part = jnp.dot((a * a).astype(BF16), wd_ref[c0:c0 + MLP_SUB, :].astype(BF16),
                           preferred_element_type=F32)
            if first and c0 == 0:
                o_ref[...] = part
            else:
                o_ref[...] += part

    @pl.when(pl.program_id(1) == 0)
    def _():
        tile(True)

    @pl.when(pl.program_id(1) > 0)
    def _():
        tile(False)


def _mlp(hn, w_up_tiles, w_down):
    rows, d = hn.shape
    n_tiles, _, bf = w_up_tiles.shape
    bm = BM_MLP
    return pl.pallas_call(
        _mlp_kernel,
        grid=(rows // bm, n_tiles),
        in_specs=[pl.BlockSpec((bm, d), lambda i, j: (i, 0)),
                  pl.BlockSpec((None, d, bf), lambda i, j: (j, 0, 0)),
                  pl.BlockSpec((bf, d), lambda i, j: (j, 0))],
        out_specs=pl.BlockSpec((bm, d), lambda i, j: (i, 0)),
        out_shape=jax.ShapeDtypeStruct((rows, d), F32),
        compiler_params=pltpu.CompilerParams(dimension_semantics=("arbitrary", "arbitrary"),
                                             vmem_limit_bytes=VMEM_LIMIT),
        name="sqrelu_mlp",
    )(hn, w_up_tiles, w_down)


def _ple_kernel(h_ref, m_ref, p_ref, gm_ref, gg_ref, wg_ref, wp_ref, gpost_ref, o_ref):
    h = h_ref[...] + _rms(m_ref[...], gm_ref[...])
    hn = _rms(h, gg_ref[...]).astype(BF16)
    gate = jax.nn.sigmoid(jnp.dot(hn, wg_ref[...], preferred_element_type=F32))
    e = jnp.dot(p_ref[...].astype(BF16), wp_ref[...].astype(BF16), preferred_element_type=F32)
    o_ref[...] = h + _rms(gate * e, gpost_ref[...])


def _ple(h1, m, p2, g_post_mlp, g_gate, w_gate_bf, w_ple, g_post):
    rows, d = h1.shape
    bm = BM_PLE
    row_blk = lambda w: pl.BlockSpec((bm, w), lambda i: (i, 0))
    return pl.pallas_call(
        _ple_kernel,
        grid=(rows // bm,),
        in_specs=[row_blk(d), row_blk(d), row_blk(p2.shape[1]),
                  _resident((1, d)), _resident((1, d)), _resident(w_gate_bf.shape), _resident(w_ple.shape),
                  _resident((1, d))],
        out_specs=row_blk(d),
        out_shape=jax.ShapeDtypeStruct((rows, d), F32),
        compiler_params=pltpu.CompilerParams(dimension_semantics=("arbitrary",),
                                             vmem_limit_bytes=VMEM_LIMIT),
        name="ple_gate",
    )(h1, m, p2, g_post_mlp, g_gate, w_gate_bf, w_ple, g_post)


def kernel(x, p, g_pre_mix, w_in, b_f, w_pool, pool_scale, g_pool_out, g_attn_out, w_out, g_post_mix,
           g_pre_mlp, w_up, w_down, g_post_mlp, g_ple_gate, w_gate, w_ple, g_post_ple):
    n_batch, seq, d = x.shape
    depth = w_in.shape[0]
    pool_width = pool_scale.shape[-1]
    attn_width = g_attn_out.shape[-1]
    n_heads = b_f.shape[-1]
    rows = n_batch * seq
    vec = lambda a: a.reshape(1, -1)

    h = x.reshape(rows, d)
    for i in range(depth):
        bias_f = jnp.pad(jnp.tile(b_f[i], FGATE_PIECES), (0, LANES - FGATE_PIECES * n_heads)).reshape(1, LANES)
        w_in_t = jnp.swapaxes(w_in[i], 0, 1)
        hn0, f_aug = _prenorm(h, vec(g_pre_mix[i]), w_in_t, bias_f, seq=seq,
                              gate_row0=pool_width + 3 * attn_width, n_heads=n_heads)
        u_tiles, qk_heads, v_t, w_out_bf, w_gate_bf = _inproj(hn0, w_in_t, w_out[i], w_gate[i],
                                                              pool_width=pool_width, attn_width=attn_width)
        y_heads, w_up_tiles = _attention(qk_heads, v_t, f_aug, w_up[i], n_batch=n_batch, n_heads=n_heads)
        h1, hn = _mix(u_tiles, y_heads, h, w_pool[i], vec(pool_scale[i]),
                      vec(g_pool_out[i]), vec(g_attn_out[i]), w_out_bf, vec(g_post_mix[i]),
                      vec(g_pre_mlp[i]), seq=seq)
        m = _mlp(hn, w_up_tiles, w_down[i])
        h = _ple(h1, m, p[i].reshape(rows, -1), vec(g_post_mlp[i]), vec(g_ple_gate[i]), w_gate_bf,
                 w_ple[i], vec(g_post_ple[i]))
    return h.reshape(n_batch, seq, d)
```

```python
import functools

import jax
import jax.numpy as jnp
from jax import lax
from jax.experimental import pallas as pl
from jax.experimental.pallas import tpu as pltpu

F32 = jnp.float32
BF16 = jnp.bfloat16

EPS = 1e-6
HEAD_DIM = 128
POOL_WINDOWS = (2, 4, 8, 16)
LANES = 128
SUBLANES = 8
HALO = 16
LOG2E = 1.4426950408889634
QUERY_SCALE = HEAD_DIM ** -0.5 * LOG2E
FGATE_PIECES = 3

BM_NORM = 1024
BM_PROJ = 2048
BN_PROJ = 512
CUMSUM_BLOCK = 256
BM_MIX = 512
ROW_CHUNK = 256
BM_MLP = 1024
BF_MLP = 1024
BM_PLE = 512
BQ = 512
HEADS_PER_STEP = 2

VMEM_LIMIT = 56 * 1024 * 1024


def _rms(x, g):
    var = jnp.mean(x * x, axis=-1, keepdims=True)
    return x * lax.rsqrt(var + EPS) * g


def _log_sigmoid(x):
    return jnp.minimum(x, 0.0) - jnp.log1p(jnp.exp(-jnp.abs(x)))


def _resident(shape):
    return pl.BlockSpec(shape, lambda *_: (0,) * len(shape), pipeline_mode=pl.Buffered(1))


def _pipelined_call(body, *, grid, in_specs, out_specs, out_shape, name):
    in_specs, out_specs = list(in_specs), list(out_specs)

    def outer(*refs):
        pltpu.emit_pipeline(body, grid=grid, in_specs=in_specs, out_specs=out_specs)(*refs)

    whole = pl.BlockSpec(memory_space=pl.ANY)
    return pl.pallas_call(
        outer,
        in_specs=[whole] * len(in_specs),
        out_specs=[whole] * len(out_specs),
        out_shape=out_shape,
        compiler_params=pltpu.CompilerParams(vmem_limit_bytes=VMEM_LIMIT),
        name=name,
    )


def _dot_t(a, b_t):
    return lax.dot_general(a, b_t, (((1,), (1,)), ((), ())), preferred_element_type=F32)


def _split3(v):
    hi = v.astype(BF16)
    rem = v - hi.astype(F32)
    mid = rem.astype(BF16)
    lo = (rem - mid.astype(F32)).astype(BF16)
    return hi, mid, lo


def _prenorm_kernel(x_ref, g_ref, wf_ref, bf_ref, hn_ref, fa_ref, carry_ref, *, blocks_per_seq, n_heads):
    i = pl.program_id(0)
    bm, d = x_ref.shape
    hn = _rms(x_ref[...], g_ref[...]).astype(BF16)
    hn_ref[...] = hn
    wf = wf_ref[...]
    wf_rep = jnp.concatenate([wf] * FGATE_PIECES + [jnp.zeros((LANES - FGATE_PIECES * n_heads, d), F32)],
                             axis=0).astype(BF16)
    logf = _log_sigmoid(_dot_t(hn, wf_rep) + bf_ref[...])
    n = CUMSUM_BLOCK
    row = lax.broadcasted_iota(jnp.int32, (n, n), 0)
    col = lax.broadcasted_iota(jnp.int32, (n, n), 1)
    tri = (col <= row).astype(BF16)
    lane = lax.broadcasted_iota(jnp.int32, (n, LANES), 1)
    carry = jnp.where(i % blocks_per_seq == 0, 0.0, carry_ref[...])
    for sb in range(bm // n):
        hi, mid, lo = _split3(logf[sb * n:(sb + 1) * n, :])
        cs = (jnp.dot(tri, hi, preferred_element_type=F32)
              + jnp.dot(tri, mid, preferred_element_type=F32)
              + jnp.dot(tri, lo, preferred_element_type=F32)) + carry
        carry = cs[n - 1:n, :]
        hi, mid, lo = _split3(cs * (-LOG2E))
        zero = jnp.zeros_like(hi)
        fa_ref[sb * n:(sb + 1) * n, :] = jnp.where(
            lane < n_heads, hi, jnp.where(lane < 2 * n_heads, mid, jnp.where(lane < 3 * n_heads, lo, zero)))
    carry_ref[...] = carry


def _prenorm(x2, g, w_in_t, b_f, *, seq, gate_row0, n_heads):
    rows, d = x2.shape
    bm = BM_NORM
    assert n_heads == SUBLANES and FGATE_PIECES * n_heads <= LANES
    row_blk = lambda w: pl.BlockSpec((bm, w), lambda i: (i, 0))
    kern = functools.partial(_prenorm_kernel, blocks_per_seq=seq // bm, n_heads=n_heads)
    return pl.pallas_call(
        kern,
        grid=(rows // bm,),
        in_specs=[row_blk(d), _resident((1, d)),
                  pl.BlockSpec((n_heads, d), lambda i: (gate_row0 // n_heads, 0)),
                  _resident((1, LANES))],
        out_specs=[row_blk(d), row_blk(LANES)],
        out_shape=[jax.ShapeDtypeStruct((rows, d), BF16), jax.ShapeDtypeStruct((rows, LANES), BF16)],
        scratch_shapes=[pltpu.VMEM((1, LANES), F32)],
        compiler_params=pltpu.CompilerParams(dimension_semantics=("arbitrary",),
                                             vmem_limit_bytes=VMEM_LIMIT),
        name="prenorm_gates",
    )(x2, g, w_in_t, b_f)


def _inproj_kernel(hn_ref, w_ref, wo_ref, wg_ref, u_ref, qk_ref, vt_ref, wo_bf_ref, wg_bf_ref,
                   *, u_tiles, q_tiles):
    j = pl.program_id(1)
    v_start = u_tiles + 2 * q_tiles

    wo_bf_ref[...] = wo_ref[...].astype(BF16)
    wg_bf_ref[...] = wg_ref[...].astype(BF16)

    @pl.when(j < u_tiles)
    def _():
        u_ref[...] = _dot_t(hn_ref[...], w_ref[...].astype(BF16))

    @pl.when((j >= u_tiles) & (j < v_start))
    def _():
        c = jnp.where(j < u_tiles + q_tiles, QUERY_SCALE, 1.0)
        z = (_dot_t(hn_ref[...], w_ref[...].astype(BF16)) * c).astype(BF16)
        for hh in range(qk_ref.shape[0]):
            qk_ref[hh] = z[:, hh * HEAD_DIM:(hh + 1) * HEAD_DIM]

    @pl.when(j >= v_start)
    def _():
        vt_ref[...] = _dot_t(w_ref[...].astype(BF16), hn_ref[...]).astype(BF16)


def _inproj(hn, w_in_t, w_out, w_gate, *, pool_width, attn_width):
    rows, d = hn.shape
    bm, bn = BM_PROJ, BN_PROJ
    u_tiles = pool_width // bn
    q_tiles = attn_width // bn
    heads_per_tile = bn // HEAD_DIM
    n_tiles = (pool_width + 3 * attn_width) // bn
    n_steps = (rows // bm) * n_tiles
    cast_rows = w_out.shape[0] // n_steps
    cast_blk = pl.BlockSpec((cast_rows, w_out.shape[1]), lambda i, j: (i * n_tiles + j, 0))
    kern = functools.partial(_inproj_kernel, u_tiles=u_tiles, q_tiles=q_tiles)
    return _pipelined_call(
        kern,
        grid=(rows // bm, n_tiles),
        in_specs=[pl.BlockSpec((bm, d), lambda i, j: (i, 0), pipeline_mode=pl.Buffered(2, use_lookahead=True)),
                  pl.BlockSpec((bn, d), lambda i, j: (j, 0)),
                  cast_blk, cast_blk],
        out_specs=[pl.BlockSpec((None, bm, bn), lambda i, j: (jnp.minimum(j, u_tiles - 1), i, 0)),
                   pl.BlockSpec((heads_per_tile, bm, HEAD_DIM),
                                lambda i, j: (jnp.clip(j - u_tiles, 0, 2 * q_tiles - 1), i, 0)),
                   pl.BlockSpec((bn, bm), lambda i, j: (jnp.maximum(j - u_tiles - 2 * q_tiles, 0), i)),
                   cast_blk, cast_blk],
        out_shape=[jax.ShapeDtypeStruct((u_tiles, rows, bn), F32),
                   jax.ShapeDtypeStruct((2 * attn_width // HEAD_DIM, rows, HEAD_DIM), BF16),
                   jax.ShapeDtypeStruct((attn_width, rows), BF16),
                   jax.ShapeDtypeStruct(w_out.shape, BF16),
                   jax.ShapeDtypeStruct(w_gate.shape, BF16)],
        name="inproj",
    )(hn, w_in_t, w_out, w_gate)


def _attn_kernel(q_ref, k_ref, vt_ref, fa_ref, wu_ref, o_ref, wu_bf_ref, kaug_ref, *s_refs, n_heads):
    heads_here, seq, _ = q_ref.shape
    n_blk = seq // BQ
    wu_bf_ref[...] = wu_ref[...].astype(BF16)
    lane = lax.broadcasted_iota(jnp.int32, (BQ, LANES), 1)
    krow = lax.broadcasted_iota(jnp.int32, (BQ, BQ), 0)
    qcol = lax.broadcasted_iota(jnp.int32, (BQ, BQ), 1)
    causal = krow <= qcol
    maxes = []
    for hh in range(heads_here):
        h = pl.program_id(1) * heads_here + hh
        kaug_ref[hh, :, 0:HEAD_DIM] = k_ref[hh]
        kaug_ref[hh, :, HEAD_DIM:] = fa_ref[...]
        pick = (lane == h) | (lane == h + n_heads) | (lane == h + 2 * n_heads)
        onehot = jnp.where(pick, 1.0, 0.0).astype(BF16)
        for blk in range(n_blk):
            start, end = blk * BQ, (blk + 1) * BQ
            s_ref = s_refs[hh * n_blk + blk]
            q_aug = jnp.concatenate([q_ref[hh, start:end, :], onehot], axis=1)
            s = _dot_t(kaug_ref[hh, 0:end, :], q_aug)
            s_diag = jnp.where(causal, s[start:end, :], -jnp.inf)
            s_ref[start:end, :] = s_diag
            m = jnp.max(s_diag, axis=0, keepdims=True)
            if start:
                s_ref[0:start, :] = s[0:start, :]
                m = jnp.maximum(m, jnp.max(s[0:start, :], axis=0, keepdims=True))
            maxes.append(m)
    for hh in range(heads_here):
        for blk in range(n_blk):
            s_ref, m = s_refs[hh * n_blk + blk], maxes[hh * n_blk + blk]
            denom = jnp.zeros((1, BQ), F32)
            o_t = jnp.zeros((HEAD_DIM, BQ), F32)
            for c in range(blk + 1):
                keys = slice(c * BQ, (c + 1) * BQ)
                p = jnp.exp2(s_ref[keys, :] - m)
                denom = denom + jnp.sum(p, axis=0, keepdims=True)
                o_t = o_t + jnp.dot(vt_ref[hh * HEAD_DIM:(hh + 1) * HEAD_DIM, keys], p.astype(BF16),
                                    preferred_element_type=F32)
            o_ref[hh, blk * BQ:(blk + 1) * BQ, :] = (o_t / denom).T


def _attention(qk_heads, v_t, f_aug, w_up, *, n_batch, n_heads):
    _, rows, _ = qk_heads.shape
    seq = rows // n_batch
    d, d_ff = w_up.shape
    n_tiles = d_ff // BF_MLP
    hps = HEADS_PER_STEP
    groups = n_heads // hps
    parts = (n_batch * groups) // n_tiles
    part_rows = d // parts
    head_blk = lambda part: pl.BlockSpec((hps, seq, HEAD_DIM), lambda b, g: (part * groups + g, b, 0))
    step = lambda b, g: b * groups + g
    return pl.pallas_call(
        functools.partial(_attn_kernel, n_heads=n_heads),
        grid=(n_batch, groups),
        in_specs=[head_blk(0), head_blk(1),
                  pl.BlockSpec((hps * HEAD_DIM, seq), lambda b, g: (g, b)),
                  pl.BlockSpec((seq, LANES), lambda b, g: (b, 0)),
                  pl.BlockSpec((part_rows, BF_MLP), lambda b, g: (step(b, g) % parts, step(b, g) // parts))],
        out_specs=[head_blk(0),
                   pl.BlockSpec((None, part_rows, BF_MLP), lambda b, g: (step(b, g) // parts, step(b, g) % parts, 0))],
        out_shape=[jax.ShapeDtypeStruct((n_heads, rows, HEAD_DIM), F32),
                   jax.ShapeDtypeStruct((n_tiles, d, BF_MLP), BF16)],
        scratch_shapes=[pltpu.VMEM((hps, seq, HEAD_DIM + LANES), BF16)]
                       + [pltpu.VMEM(((blk + 1) * BQ, BQ), F32) for _ in range(hps) for blk in range(seq // BQ)],
        compiler_params=pltpu.CompilerParams(dimension_semantics=("arbitrary", "arbitrary"),
                                             vmem_limit_bytes=VMEM_LIMIT),
        name="fox_attention",
    )(qk_heads, qk_heads, v_t, f_aug, w_up)


def _mix_kernel(u_ref, halo_ref, ya_ref, x_ref, wp_ref, ps_ref, gp_ref, ga_ref, wo_ref, gm_ref, gn_ref,
                h_ref, hn_ref, *, blocks_per_seq, pool_group):
    i = pl.program_id(0)
    u_tiles, bm, bn = u_ref.shape
    pool_width = u_tiles * bn
    first = (i % blocks_per_seq) == 0
    wp = [wp_ref[gi].astype(BF16) for gi in range(len(POOL_WINDOWS))]

    for r0 in range(0, bm, ROW_CHUNK):
        rows = slice(r0, r0 + ROW_CHUNK)
        pos = ((i % blocks_per_seq) * bm + r0
               + lax.broadcasted_iota(jnp.int32, (ROW_CHUNK, 1), 0)).astype(F32)
        groups = []
        for gi, w in enumerate(POOL_WINDOWS):
            c0 = gi * pool_group
            t, cols = c0 // bn, slice(c0 % bn, c0 % bn + pool_group)
            cur = u_ref[t, rows, cols]
            if r0:
                back = u_ref[t, r0 - HALO:r0, cols]
            else:
                back = jnp.where(first, 0.0, halo_ref[t, :, cols])
            acc = jnp.concatenate([back, cur], axis=0)
            span = 1
            while span < w:
                acc = acc + pltpu.roll(acc, span, axis=0)
                span *= 2
            acc = acc[HALO:, :]
            count = jnp.minimum(pos + 1.0, float(w))
            y = (acc / count - cur).astype(BF16)
            y = jnp.dot(y, wp[gi], preferred_element_type=F32)
            groups.append(y * ps_ref[:, c0:c0 + pool_group])
        y_pool = jnp.concatenate(groups, axis=-1)
        y_attn = jnp.concatenate([ya_ref[hh, rows, :] for hh in range(ya_ref.shape[0])], axis=-1)

        n_pool = _rms(y_pool, gp_ref[...]).astype(BF16)
        n_attn = _rms(y_attn, ga_ref[...]).astype(BF16)
        mixed = (jnp.dot(n_pool, wo_ref[0:pool_width, :], preferred_element_type=F32)
                 + jnp.dot(n_attn, wo_ref[pool_width:, :], preferred_element_type=F32))
        h1 = x_ref[rows, :] + _rms(mixed, gm_ref[...])
        h_ref[rows, :] = h1
        hn_ref[rows, :] = _rms(h1, gn_ref[...]).astype(BF16)


def _mix(u_tiles, y_heads, x2, w_pool, pool_scale, g_pool, g_attn, w_out_bf, g_post, g_next, *, seq):
    rows, d = x2.shape
    n_u, _, bn = u_tiles.shape
    n_heads = y_heads.shape[0]
    pool_width = n_u * bn
    attn_width = n_heads * HEAD_DIM
    bm = BM_MIX
    blocks_per_seq = seq // bm
    assert all(w & (w - 1) == 0 and w <= HALO for w in POOL_WINDOWS)
    row_blk = lambda w: pl.BlockSpec((bm, w), lambda i: (i, 0))
    halo_blk = pl.BlockSpec((n_u, HALO, bn), lambda i: (0, jnp.maximum(i * (bm // HALO) - 1, 0), 0))
    kern = functools.partial(_mix_kernel, blocks_per_seq=blocks_per_seq, pool_group=w_pool.shape[-1])
    return pl.pallas_call(
        kern,
        grid=(rows // bm,),
        in_specs=[pl.BlockSpec((n_u, bm, bn), lambda i: (0, i, 0)), halo_blk,
                  pl.BlockSpec((n_heads, bm, HEAD_DIM), lambda i: (0, i, 0)), row_blk(d),
                  _resident(w_pool.shape), _resident((1, pool_width)), _resident((1, pool_width)),
                  _resident((1, attn_width)), _resident(w_out_bf.shape), _resident((1, d)), _resident((1, d))],
        out_specs=[row_blk(d), row_blk(d)],
        out_shape=[jax.ShapeDtypeStruct((rows, d), F32), jax.ShapeDtypeStruct((rows, d), BF16)],
        compiler_params=pltpu.CompilerParams(dimension_semantics=("arbitrary",),
                                             vmem_limit_bytes=VMEM_LIMIT),
        name="mix_out",
    )(u_tiles, u_tiles, y_heads, x2, w_pool, pool_scale, g_pool, g_attn, w_out_bf, g_post, g_next)


def _mlp_kernel(hn_ref, wu_ref, wd_ref, o_ref):
    def tile(first):
        a = jnp.maximum(jnp.dot(hn_ref[...], wu_ref[...], preferred_element_type=F32), 0.0)
        part = jnp.dot((a * a).astype(BF16), wd_ref[...].astype(BF16), preferred_element_type=F32)
        if first:
            o_ref[...] = part
        else:
            o_ref[...] += part

    @pl.when(pl.program_id(1) == 0)
    def _():
        tile(True)

    @pl.when(pl.program_id(1) > 0)
    def _():
        tile(False)


def _mlp(hn, w_up_tiles, w_down):
    rows, d = hn.shape
    n_tiles, _, bf = w_up_tiles.shape
    bm = BM_MLP
    return pl.pallas_call(
        _mlp_kernel,
        grid=(rows // bm, n_tiles),
        in_specs=[pl.BlockSpec((bm, d), lambda i, j: (i, 0)),
                  pl.BlockSpec((None, d, bf), lambda i, j: (j, 0, 0)),
                  pl.BlockSpec((bf, d), lambda i, j: (j, 0))],
        out_specs=pl.BlockSpec((bm, d), lambda i, j: (i, 0)),
        out_shape=jax.ShapeDtypeStruct((rows, d), F32),
        compiler_params=pltpu.CompilerParams(dimension_semantics=("arbitrary", "arbitrary"),
                                             vmem_limit_bytes=VMEM_LIMIT),
        name="sqrelu_mlp",
    )(hn, w_up_tiles, w_down)


def _ple_kernel(h_ref, m_ref, p_ref, gm_ref, gg_ref, wg_ref, wp_ref, gpost_ref, o_ref):
    h = h_ref[...] + _rms(m_ref[...], gm_ref[...])
    hn = _rms(h, gg_ref[...]).astype(BF16)
    gate = jax.nn.sigmoid(jnp.dot(hn, wg_ref[...], preferred_element_type=F32))
    e = jnp.dot(p_ref[...].astype(BF16), wp_ref[...].astype(BF16), preferred_element_type=F32)
    o_ref[...] = h + _rms(gate * e, gpost_ref[...])


def _ple(h1, m, p2, g_post_mlp, g_gate, w_gate_bf, w_ple, g_post):
    rows, d = h1.shape
    bm = BM_PLE
    row_blk = lambda w: pl.BlockSpec((bm, w), lambda i: (i, 0))
    return pl.pallas_call(
        _ple_kernel,
        grid=(rows // bm,),
        in_specs=[row_blk(d), row_blk(d), row_blk(p2.shape[1]),
                  _resident((1, d)), _resident((1, d)), _resident(w_gate_bf.shape), _resident(w_ple.shape),
                  _resident((1, d))],
        out_specs=row_blk(d),
        out_shape=jax.ShapeDtypeStruct((rows, d), F32),
        compiler_params=pltpu.CompilerParams(dimension_semantics=("arbitrary",),
                                             vmem_limit_bytes=VMEM_LIMIT),
        name="ple_gate",
    )(h1, m, p2, g_post_mlp, g_gate, w_gate_bf, w_ple, g_post)


def kernel(x, p, g_pre_mix, w_in, b_f, w_pool, pool_scale, g_pool_out, g_attn_out, w_out, g_post_mix,
           g_pre_mlp, w_up, w_down, g_post_mlp, g_ple_gate, w_gate, w_ple, g_post_ple):
    n_batch, seq, d = x.shape
    depth = w_in.shape[0]
    pool_width = pool_scale.shape[-1]
    attn_width = g_attn_out.shape[-1]
    n_heads = b_f.shape[-1]
    rows = n_batch * seq
    vec = lambda a: a.reshape(1, -1)

    h = x.reshape(rows, d)
    for i in range(depth):
        bias_f = jnp.pad(jnp.tile(b_f[i], FGATE_PIECES), (0, LANES - FGATE_PIECES * n_heads)).reshape(1, LANES)
        w_in_t = jnp.swapaxes(w_in[i], 0, 1)
        hn0, f_aug = _prenorm(h, vec(g_pre_mix[i]), w_in_t, bias_f, seq=seq,
                              gate_row0=pool_width + 3 * attn_width, n_heads=n_heads)
        u_tiles, qk_heads, v_t, w_out_bf, w_gate_bf = _inproj(hn0, w_in_t, w_out[i], w_gate[i],
                                                              pool_width=pool_width, attn_width=attn_width)
        y_heads, w_up_tiles = _attention(qk_heads, v_t, f_aug, w_up[i], n_batch=n_batch, n_heads=n_heads)
        h1, hn = _mix(u_tiles, y_heads, h, w_pool[i], vec(pool_scale[i]),
                      vec(g_pool_out[i]), vec(g_attn_out[i]), w_out_bf, vec(g_post_mix[i]),
                      vec(g_pre_mlp[i]), seq=seq)
        m = _mlp(hn, w_up_tiles, w_down[i])
        h = _ple(h1, m, p[i].reshape(rows, -1), vec(g_post_mlp[i]), vec(g_ple_gate[i]), w_gate_bf,
                 w_ple[i], vec(g_post_ple[i]))
    return h.reshape(n_batch, seq, d)
```
